```python
import math
import jax, jax.numpy as jnp
from jax import lax
import numpy as np

D_MODEL = 1024
BATCH = 8
SEQ = 4096
DEPTH = 2

HEAD_DIM = 64
MOBA_HEADS = 4
DIFF_HEADS = 4
DSA_HEADS = 4
DIFF_V_DIM = 2 * HEAD_DIM
IDX_HEADS = 8
IDX_DIM = 64
MOBA_BLOCK = 256
MOBA_TOPK = 3
DSA_TOPK_MAX = 256
ROPE_THETA = 10000.0
EPS = 1e-6
Q_BLOCK = 128
MOBA_Q_CHUNK = 32
D_FF = -(-8 * D_MODEL // (3 * 256)) * 256

MOBA_W = MOBA_HEADS * HEAD_DIM
DIFF_QK_W = DIFF_HEADS * 2 * HEAD_DIM
DIFF_V_W = DIFF_HEADS * DIFF_V_DIM
DSA_W = DSA_HEADS * HEAD_DIM
IDX_Q_W = IDX_HEADS * IDX_DIM
MIX_W = MOBA_W + DIFF_V_W + DSA_W
IN_COLS = 3 * MOBA_W + 2 * DIFF_QK_W + DIFF_V_W + 3 * DSA_W + IDX_Q_W + IDX_DIM + IDX_HEADS

kernel_name = "hybrid_moba_diff_dsa_block"


def rms_norm(x, g):
    xf = x.astype(jnp.float32)
    y = xf * lax.rsqrt(jnp.mean(xf * xf, axis=-1, keepdims=True) + EPS)
    return (y * g.astype(jnp.float32)).astype(x.dtype)


def rope_tables(seq, dim):
    inv = 1.0 / (ROPE_THETA ** (jnp.arange(0, dim, 2, dtype=jnp.float32) / dim))
    ang = jnp.arange(seq, dtype=jnp.float32)[:, None] * inv[None, :]
    return jnp.cos(ang), jnp.sin(ang)


def apply_rope(x, cos, sin):
    x1, x2 = jnp.split(x, 2, axis=-1)
    c = cos.astype(x.dtype)
    s = sin.astype(x.dtype)
    return jnp.concatenate([x1 * c - x2 * s, x1 * s + x2 * c], axis=-1)


def split_heads(t, n_heads):
    b, s, _ = t.shape
    return t.reshape(b, s, n_heads, -1).transpose(0, 2, 1, 3)


def merge_heads(t):
    b, h, s, d = t.shape
    return t.transpose(0, 2, 1, 3).reshape(b, s, h * d)


def moba_attention(q, k, v):
    B, H, S, D = q.shape
    nb = -(-S // MOBA_BLOCK)
    pad = nb * MOBA_BLOCK - S
    kb = jnp.pad(k, ((0, 0), (0, 0), (0, pad), (0, 0))).reshape(B, H, nb, MOBA_BLOCK, D)
    vb = jnp.pad(v, ((0, 0), (0, 0), (0, pad), (0, 0))).reshape(B, H, nb, MOBA_BLOCK, D)
    k_mean = jnp.mean(kb.astype(jnp.float32), axis=3).astype(k.dtype)
    n_sel = min(MOBA_TOPK, nb - 1)
    scale = D ** -0.5
    blk_ids = jnp.arange(nb)
    own_off = jnp.arange(MOBA_BLOCK)
    gather = jax.vmap(jax.vmap(lambda t, i: t[i]))

    def chunk(i):
        start = i * MOBA_Q_CHUNK
        qc = lax.dynamic_slice_in_dim(q, start, MOBA_Q_CHUNK, axis=2)
        qpos = start + jnp.arange(MOBA_Q_CHUNK)
        cur = start // MOBA_BLOCK
        k_own = lax.dynamic_index_in_dim(kb, cur, axis=2, keepdims=False)
        v_own = lax.dynamic_index_in_dim(vb, cur, axis=2, keepdims=False)
        own_pos = cur * MOBA_BLOCK + own_off
        s_own = jnp.einsum('bhcd,bhkd->bhck', qc, k_own).astype(jnp.float32) * scale
        s_own = jnp.where(own_pos[None, :] <= qpos[:, None], s_own, -jnp.inf)
        if n_sel == 0:
            p_own = jax.nn.softmax(s_own, axis=-1).astype(v.dtype)
            return jnp.einsum('bhck,bhkd->bhcd', p_own, v_own)
        gate = jnp.einsum('bhcd,bhnd->bhcn', qc, k_mean).astype(jnp.float32)
        gate = jnp.where(blk_ids < cur, gate, -jnp.inf)
        _, sel = lax.top_k(gate, n_sel)
        sel_ok = jnp.arange(n_sel) < cur
        k_sel = gather(kb, sel)
        v_sel = gather(vb, sel)
        s_sel = jnp.einsum('bhcd,bhcnkd->bhcnk', qc, k_sel).astype(jnp.float32) * scale
        s_sel = jnp.where(sel_ok[:, None], s_sel, -jnp.inf).reshape(B, H, MOBA_Q_CHUNK, n_sel * MOBA_BLOCK)
        p = jax.nn.softmax(jnp.concatenate([s_sel, s_own], axis=-1), axis=-1).astype(v.dtype)
        p_sel = p[..., :n_sel * MOBA_BLOCK].reshape(B, H, MOBA_Q_CHUNK, n_sel, MOBA_BLOCK)
        p_own = p[..., n_sel * MOBA_BLOCK:]
        return (jnp.einsum('bhcnk,bhcnkd->bhcd', p_sel, v_sel)
                + jnp.einsum('bhck,bhkd->bhcd', p_own, v_own))

    out = lax.map(chunk, jnp.arange(S // MOBA_Q_CHUNK))
    return out.transpose(1, 2, 0, 3, 4).reshape(B, H, S, D)


def diff_attention(q1, q2, k1, k2, v, lam):
    B, H, S, D = q1.shape
    scale = D ** -0.5
    kpos = jnp.arange(S)

    def block(i):
        start = i * Q_BLOCK
        qpos = start + jnp.arange(Q_BLOCK)
        mask = kpos[None, :] <= qpos[:, None]

        def probs(qf, kf):
            qb = lax.dynamic_slice_in_dim(qf, start, Q_BLOCK, axis=2)
            s = jnp.einsum('bhqd,bhkd->bhqk', qb, kf).astype(jnp.float32) * scale
            return jax.nn.softmax(jnp.where(mask, s, -jnp.inf), axis=-1)

        a = probs(q1, k1) - lam * probs(q2, k2)
        return jnp.einsum('bhqk,bhkv->bhqv', a.astype(v.dtype), v)

    out = lax.map(block, jnp.arange(S // Q_BLOCK))
    return out.transpose(1, 2, 0, 3, 4).reshape(B, H, S, v.shape[-1])


def dsa_attention(q, k, v, iq, ik, iw):
    B, H, S, D = q.shape
    n_top = min(DSA_TOPK_MAX, S // 4)
    scale = D ** -0.5
    kpos = jnp.arange(S)
    gather = jax.vmap(lambda t, i: t[:, i])

    def chunk(i):
        start = i * Q_BLOCK
        qpos = start + jnp.arange(Q_BLOCK)
        iqc = lax.dynamic_slice_in_dim(iq, start, Q_BLOCK, axis=2)
        iwc = lax.dynamic_slice_in_dim(iw, start, Q_BLOCK, axis=1)
        logits = jnp.einsum('bhcd,bsd->bhcs', iqc, ik).astype(jnp.float32) * (IDX_DIM ** -0.5)
        score = jnp.einsum('bch,bhcs->bcs', iwc.astype(jnp.float32), jax.nn.relu(logits))
        score = jnp.where(kpos[None, :] <= qpos[:, None], score, -jnp.inf)
        _, sel = lax.top_k(score, n_top)
        sel_ok = sel <= qpos[None, :, None]
        qc = lax.dynamic_slice_in_dim(q, start, Q_BLOCK, axis=2)
        k_sel = gather(k, sel)
        v_sel = gather(v, sel)
        s = jnp.einsum('bhcd,bhcnd->bhcn', qc, k_sel).astype(jnp.float32) * scale
        s = jnp.where(sel_ok[:, None], s, -jnp.inf)
        p = jax.nn.softmax(s, axis=-1).astype(v.dtype)
        return jnp.einsum('bhcn,bhcnd->bhcd', p, v_sel)

    out = lax.map(chunk, jnp.arange(S // Q_BLOCK))
    return out.transpose(1, 2, 0, 3, 4).reshape(B, H, S, D)


def setup_inputs(seed: int = 0) -> dict:
    key = jax.random.key(seed)
    ks = jax.random.split(key, 12)
    f32 = jnp.float32

    def nrm(k, shape, scale):
        return jax.random.normal(k, shape, f32) * scale

    return {
        "x": nrm(ks[0], (BATCH, SEQ, D_MODEL), 1.0),
        "attn_norm_g": 1.0 + nrm(ks[1], (DEPTH, D_MODEL), 0.05),
        "w_in": nrm(ks[2], (DEPTH, D_MODEL, IN_COLS), D_MODEL ** -0.5),
        "q_norm_g": 1.0 + nrm(ks[3], (DEPTH, 3, HEAD_DIM), 0.05),
        "k_norm_g": 1.0 + nrm(ks[4], (DEPTH, 3, HEAD_DIM), 0.05),
        "diff_lambda": nrm(ks[5], (DEPTH, 4, HEAD_DIM), 0.1),
        "diff_subln_g": 1.0 + nrm(ks[6], (DEPTH, DIFF_V_DIM), 0.05),
        "w_out": nrm(ks[7], (DEPTH, MIX_W, D_MODEL), MIX_W ** -0.5),
        "ffn_norm_g": 1.0 + nrm(ks[8], (DEPTH, D_MODEL), 0.05),
        "w_gate": nrm(ks[9], (DEPTH, D_MODEL, D_FF), D_MODEL ** -0.5),
        "w_up": nrm(ks[10], (DEPTH, D_MODEL, D_FF), D_MODEL ** -0.5),
        "w_down": nrm(ks[11], (DEPTH, D_FF, D_MODEL), D_FF ** -0.5),
    }


def reference(x, attn_norm_g, w_in, q_norm_g, k_norm_g, diff_lambda, diff_subln_g,
              w_out, ffn_norm_g, w_gate, w_up, w_down):
    S = x.shape[1]
    cos, sin = rope_tables(S, HEAD_DIM)
    sizes = [MOBA_W] * 3 + [DIFF_QK_W, DIFF_QK_W, DIFF_V_W] + [DSA_W] * 3 + [IDX_Q_W, IDX_DIM, IDX_HEADS]
    offsets = []
    acc = 0
    for sz in sizes[:-1]:
        acc += sz
        offsets.append(acc)

    for l in range(DEPTH):
        h = rms_norm(x, attn_norm_g[l])
        proj = h @ w_in[l]
        (mq, mk, mv, dq, dk, dv, sq, sk, sv, iq, ik, iw) = jnp.split(proj, offsets, axis=-1)
        qg, kg = q_norm_g[l], k_norm_g[l]

        def qk(t, nh, g):
            return apply_rope(rms_norm(split_heads(t, nh), g), cos, sin)

        moba = moba_attention(qk(mq, MOBA_HEADS, qg[0]), qk(mk, MOBA_HEADS, kg[0]),
                              split_heads(mv, MOBA_HEADS))

        dq_h = qk(dq, 2 * DIFF_HEADS, qg[1])
        dk_h = qk(dk, 2 * DIFF_HEADS, kg[1])
        lam_init = 0.8 - 0.6 * math.exp(-0.3 * l)
        lp = diff_lambda[l].astype(jnp.float32)
        lam = jnp.exp(jnp.sum(lp[0] * lp[1])) - jnp.exp(jnp.sum(lp[2] * lp[3])) + lam_init
        diff = diff_attention(dq_h[:, 0::2], dq_h[:, 1::2], dk_h[:, 0::2], dk_h[:, 1::2],
                              split_heads(dv, DIFF_HEADS), lam)
        diff = rms_norm(diff, diff_subln_g[l]) * (1.0 - lam_init)

        iq_h = apply_rope(split_heads(iq, IDX_HEADS), cos, sin)
        ik_r = apply_rope(ik, cos, sin)
        dsa = dsa_attention(qk(sq, DSA_HEADS, qg[2]), qk(sk, DSA_HEADS, kg[2]),
                            split_heads(sv, DSA_HEADS), iq_h, ik_r,
                            iw * (IDX_HEADS ** -0.5))

        mix = jnp.concatenate([merge_heads(moba), merge_heads(diff), merge_heads(dsa)], axis=-1)
        x = x + mix @ w_out[l]

        h2 = rms_norm(x, ffn_norm_g[l])
        x = x + (jax.nn.silu(h2 @ w_gate[l]) * (h2 @ w_up[l])) @ w_down[l]
    return x
```

```python
import functools
import math

import jax
import jax.numpy as jnp
from jax import lax
from jax.experimental import pallas as pl
from jax.experimental.pallas import tpu as pltpu

D_MODEL = 1024
HEAD_DIM = 64
MOBA_BLOCK = 256
MOBA_TOPK = 3
DSA_TOPK_MAX = 256
IDX_HEADS = 8
ROPE_THETA = 10000.0
EPS = 1e-6
D_FF = 2816

LANES = 128
HALF = HEAD_DIM // 2

SLAB_MOBA_Q, SLAB_MOBA_K, SLAB_MOBA_V = 0, 2, 4
SLAB_DIFF_Q, SLAB_DIFF_K, SLAB_DIFF_V = 6, 10, 14
SLAB_DSA_Q, SLAB_DSA_K, SLAB_DSA_V = 18, 20, 22
SLAB_IQ, SLAB_IK, SLAB_IW = 24, 28, 29
N_SLABS_BF16 = 29
N_SLABS = 30
IN_COLS = 3656

ATTN_SCALE = HEAD_DIM ** -0.5
IDX_SCALE = 64 ** -0.5

VMEM_LIMIT = 56 * 1024 * 1024
ACT_DTYPE = jnp.bfloat16

NEG_INF = float("-inf")
F32_LOWEST_BITS = -8388609

_NT = (((1,), (1,)), ((), ()))


def _f32(x):
    return x.astype(jnp.float32)


def _bf16(x):
    return x.astype(ACT_DTYPE)


def _slab_kind(s):
    if s < SLAB_MOBA_K:
        return True, 0, True, ATTN_SCALE
    if s < SLAB_MOBA_V:
        return True, 1, True, 1.0
    if s < SLAB_DIFF_Q:
        return False, 0, False, 1.0
    if s < SLAB_DIFF_K:
        return True, 2, True, ATTN_SCALE
    if s < SLAB_DIFF_V:
        return True, 3, True, 1.0
    if s < SLAB_DSA_Q:
        return False, 0, False, 1.0
    if s < SLAB_DSA_K:
        return True, 4, True, ATTN_SCALE
    if s < SLAB_DSA_V:
        return True, 5, True, 1.0
    if s < SLAB_IQ:
        return False, 0, False, 1.0
    if s < SLAB_IK:
        return False, 0, True, IDX_SCALE
    if s < SLAB_IW:
        return False, 0, True, 1.0
    return False, 0, False, IDX_HEADS ** -0.5


def _inproj_kernel(x_ref, g_ref, w_ref, cos_ref, sin_ref, hg_ref, p_ref, iw_ref):
    x = x_ref[...]
    ms = jnp.mean(x * x, axis=-1, keepdims=True)
    h = _bf16(x * lax.rsqrt(ms + EPS) * g_ref[...])

    lane = lax.broadcasted_iota(jnp.int32, (1, LANES), 1)
    first_half = (lane & (HEAD_DIM - 1)) < HALF
    r = lax.broadcasted_iota(jnp.int32, (LANES, LANES), 0) // HEAD_DIM
    c = lax.broadcasted_iota(jnp.int32, (LANES, LANES), 1) // HEAD_DIM
    head_ones = _bf16(jnp.where(r == c, 1.0, 0.0))
    cos = cos_ref[...]
    sin = sin_ref[...]
    hg = hg_ref[...]

    for pair in range(N_SLABS // 2):
        y2 = jnp.dot(h, w_ref[:, pair * 2 * LANES:(pair + 1) * 2 * LANES],
                     preferred_element_type=jnp.float32)
        for half in range(2):
            s = pair * 2 + half
            y = y2[:, half * LANES:(half + 1) * LANES]
            norm, grow, rope, scale = _slab_kind(s)
            if norm:
                yy = y * y
                hi = _bf16(yy)
                lo = _bf16(yy - _f32(hi))
                ss = (jnp.dot(hi, head_ones, preferred_element_type=jnp.float32)
                      + jnp.dot(lo, head_ones, preferred_element_type=jnp.float32))
                y = y * lax.rsqrt(ss * (1.0 / HEAD_DIM) + EPS) * hg[grow:grow + 1, :]
            if rope:
                partner = jnp.where(first_half, pltpu.roll(y, LANES - HALF, 1), pltpu.roll(y, HALF, 1))
                y = y * cos + partner * sin
            if scale != 1.0:
                y = y * scale
            if s < N_SLABS_BF16:
                p_ref[:, s * LANES:(s + 1) * LANES] = _bf16(y)
            else:
                iw_ref[...] = y


def _inproj(x2d, g, w_packed, cos_t, sin_t, head_gains, seq, tm):
    m = x2d.shape[0]
    nc = w_packed.shape[1]
    n_seq_tiles = seq // tm
    return pl.pallas_call(
        _inproj_kernel,
        grid=(m // tm,),
        in_specs=[
            pl.BlockSpec((tm, D_MODEL), lambda i: (i, 0)),
            pl.BlockSpec((1, D_MODEL), lambda i: (0, 0)),
            pl.BlockSpec((D_MODEL, nc), lambda i: (0, 0)),
            pl.BlockSpec((tm, LANES), lambda i: (i % n_seq_tiles, 0)),
            pl.BlockSpec((tm, LANES), lambda i: (i % n_seq_tiles, 0)),
            pl.BlockSpec((8, LANES), lambda i: (0, 0)),
        ],
        out_specs=[
            pl.BlockSpec((tm, N_SLABS_BF16 * LANES), lambda i: (i, 0)),
            pl.BlockSpec((tm, LANES), lambda i: (i, 0)),
        ],
        out_shape=[
            jax.ShapeDtypeStruct((m, N_SLABS_BF16 * LANES), ACT_DTYPE),
            jax.ShapeDtypeStruct((m, LANES), jnp.float32),
        ],
        compiler_params=pltpu.CompilerParams(
            dimension_semantics=("parallel",), vmem_limit_bytes=VMEM_LIMIT),
        name="inproj",
    )(x2d, g, w_packed, cos_t, sin_t, head_gains)


def _online_softmax_step(s, v, m_old, l_old, acc_ref, acc_idx):
    m_new = jnp.maximum(m_old, jnp.max(s, axis=-1, keepdims=True))
    m_safe = jnp.where(m_new == NEG_INF, 0.0, m_new)
    alpha = jnp.exp(m_old - m_safe)
    p = jnp.exp(s - m_safe)
    l_new = alpha * l_old + jnp.sum(p, axis=-1, keepdims=True)
    acc_ref[acc_idx] = alpha * acc_ref[acc_idx] + jnp.dot(_bf16(p), v, preferred_element_type=jnp.float32)
    return m_new, l_new


def _lane_masks():
    lane = lax.broadcasted_iota(jnp.int32, (1, LANES), 1)
    return lane < HEAD_DIM, lane >= HEAD_DIM


def _causal_tile_mask(t):
    row = lax.broadcasted_iota(jnp.int32, (t, t), 0)
    col = lax.broadcasted_iota(jnp.int32, (t, t), 1)
    return col <= row


def _moba_kernel(q_ref, k_ref, v_ref, o_ref, kmean_ref, acc_ref, *, n_blocks):
    t = MOBA_BLOCK
    qi = pl.program_id(2)

    @pl.when(qi == 0)
    def _():
        kmean_ref[...] = jnp.zeros_like(kmean_ref)
        for j in range(n_blocks):
            kb = _f32(k_ref[j * t:(j + 1) * t, :])
            kmean_ref[j:j + 1, :] = jnp.sum(kb, axis=0, keepdims=True) * (1.0 / t)

    q = q_ref[...]
    lo_mask, hi_mask = _lane_masks()
    km = kmean_ref[...]
    km_hi = _bf16(km)
    km_lo = _bf16(km - _f32(km_hi))
    col = lax.broadcasted_iota(jnp.int32, (t, LANES), 1)
    col_f = _f32(col)
    causal = _causal_tile_mask(t)
    k_own =k_ref[pl.ds(pl.multiple_of(qi * t, t), t), :]
    v_own = v_ref[pl.ds(pl.multiple_of(qi * t, t), t), :]
    n_sel = min(MOBA_TOPK, n_blocks - 1)

    outs = []
    for hd, hmask in enumerate((lo_mask, hi_mask)):
        qh = jnp.where(hmask, q, jnp.zeros_like(q))
        gate = (lax.dot_general(qh, km_hi, _NT, preferred_element_type=jnp.float32)
                + lax.dot_general(qh, km_lo, _NT, preferred_element_type=jnp.float32))
        gate = jnp.where(col < qi, gate, NEG_INF)
        sel = jnp.zeros((t, LANES), jnp.float32)
        for rank in range(n_sel):
            gmax = jnp.max(gate, axis=-1, keepdims=True)
            first = jnp.min(jnp.where(gate == gmax, col_f, float(LANES)), axis=-1, keepdims=True)
            pick = col_f == first
            sel = jnp.where(pick, jnp.maximum(sel, jnp.where(rank < qi, 1.0, 0.0)), sel)
            gate = jnp.where(pick, NEG_INF, gate)

        acc_ref[hd] = jnp.zeros((t, LANES), jnp.float32)
        m0 = jnp.full((t, 1), NEG_INF, jnp.float32)
        l0 = jnp.zeros((t, 1), jnp.float32)
        s_own = lax.dot_general(qh, k_own, _NT, preferred_element_type=jnp.float32)
        s_own = jnp.where(causal, s_own, NEG_INF)
        m1, l1 = _online_softmax_step(s_own, v_own, m0, l0, acc_ref, hd)

        def past_block(j, carry, qh=qh, sel=sel, hd=hd):
            m_old, l_old = carry
            off = pl.multiple_of(j * t, t)
            kj = k_ref[pl.ds(off, t), :]
            vj = v_ref[pl.ds(off, t), :]
            picked = jnp.max(jnp.where(col == j, sel, 0.0), axis=-1, keepdims=True)
            s = lax.dot_general(qh, kj, _NT, preferred_element_type=jnp.float32)
            s = jnp.where(picked > 0.0, s, NEG_INF)
            return _online_softmax_step(s, vj, m_old, l_old, acc_ref, hd)

        _, l_fin = lax.fori_loop(0, qi, past_block, (m1, l1))
        outs.append(acc_ref[hd] / l_fin)

    o_ref[...] = _bf16(jnp.where(lo_mask, outs[0], outs[1]))


def _moba(p3, seq):
    b = p3.shape[0]
    t = MOBA_BLOCK
    n_blocks = seq // t
    return pl.pallas_call(
        functools.partial(_moba_kernel, n_blocks=n_blocks),
        grid=(b, 2, n_blocks),
        in_specs=[
            pl.BlockSpec((None, t, LANES), lambda bi, s, qi: (bi, qi, SLAB_MOBA_Q + s)),
            pl.BlockSpec((None, seq, LANES), lambda bi, s, qi: (bi, 0, SLAB_MOBA_K + s)),
            pl.BlockSpec((None, seq, LANES), lambda bi, s, qi: (bi, 0, SLAB_MOBA_V + s)),
        ],
        out_specs=pl.BlockSpec((None, t, LANES), lambda bi, s, qi: (bi, qi, s)),
        out_shape=jax.ShapeDtypeStruct((b, seq, 2 * LANES), ACT_DTYPE),
        scratch_shapes=[
            pltpu.VMEM((LANES, LANES), jnp.float32),
            pltpu.VMEM((2, t, LANES), jnp.float32),
        ],
        compiler_params=pltpu.CompilerParams(
            dimension_semantics=("parallel", "parallel", "arbitrary"), vmem_limit_bytes=VMEM_LIMIT),
        name="moba",
    )(p3, p3, p3)


def _diff_kernel(lam_ref, q_ref, k_ref, v_ref, sg_ref, o_ref, acc_ref, *, tq, lam_init):
    qi = pl.program_id(2)
    q = q_ref[...]
    lo_mask, hi_mask = _lane_masks()
    causal = _causal_tile_mask(tq)
    zero = jnp.zeros_like(q)
    qs = (jnp.where(lo_mask, q, zero), jnp.where(hi_mask, q, zero))

    off_d = pl.multiple_of(qi * tq, tq)
    k_d = k_ref[pl.ds(off_d, tq), :]
    v_d = v_ref[pl.ds(off_d, tq), :]
    m0 = jnp.full((tq, 1), NEG_INF, jnp.float32)
    l0 = jnp.zeros((tq, 1), jnp.float32)
    carry = []
    for c in range(2):
        acc_ref[c] = jnp.zeros((tq, LANES), jnp.float32)
        s = lax.dot_general(qs[c], k_d, _NT, preferred_element_type=jnp.float32)
        s = jnp.where(causal, s, NEG_INF)
        carry.extend(_online_softmax_step(s, v_d, m0, l0, acc_ref, c))

    def past_tile(j, carry):
        off = pl.multiple_of(j * tq, tq)
        kj = k_ref[pl.ds(off, tq), :]
        vj = v_ref[pl.ds(off, tq), :]
        out = []
        for c in range(2):
            s = lax.dot_general(qs[c], kj, _NT, preferred_element_type=jnp.float32)
            out.extend(_online_softmax_step(s, vj, carry[2 * c], carry[2 * c + 1], acc_ref, c))
        return tuple(out)

    _, l1, _, l2 = lax.fori_loop(0, qi, past_tile, tuple(carry))

    lp = lam_ref[...]
    lam = (jnp.exp(jnp.sum(lp[0:1, :] * lp[1:2, :], axis=-1, keepdims=True))
           - jnp.exp(jnp.sum(lp[2:3, :] * lp[3:4, :], axis=-1, keepdims=True)) + lam_init)
    a = acc_ref[0] / l1 - lam * (acc_ref[1] / l2)
    ms = jnp.mean(a * a, axis=-1, keepdims=True)
    o_ref[...] = _bf16(a * lax.rsqrt(ms + EPS) * sg_ref[...] * (1.0 - lam_init))


def _diff(p3, lam_params, subln_g, seq, lam_init, tq):
    b = p3.shape[0]
    n_heads = 4
    return pl.pallas_call(
        functools.partial(_diff_kernel, tq=tq, lam_init=lam_init),
        grid=(b, n_heads, seq // tq),
        in_specs=[
            pl.BlockSpec((4, HEAD_DIM), lambda bi, h, qi: (0, 0)),
            pl.BlockSpec((None, tq, LANES), lambda bi, h, qi: (bi, qi, SLAB_DIFF_Q + h)),
            pl.BlockSpec((None, seq, LANES), lambda bi, h, qi: (bi, 0, SLAB_DIFF_K + h)),
            pl.BlockSpec((None, seq, LANES), lambda bi, h, qi: (bi, 0, SLAB_DIFF_V + h)),
            pl.BlockSpec((1, LANES), lambda bi, h, qi: (0, 0)),
        ],
        out_specs=pl.BlockSpec((None, tq, LANES), lambda bi, h, qi: (bi, qi, h)),
        out_shape=jax.ShapeDtypeStruct((b, seq, n_heads * LANES), ACT_DTYPE),
        scratch_shapes=[pltpu.VMEM((2, tq, LANES), jnp.float32)],
        compiler_params=pltpu.CompilerParams(
            dimension_semantics=("parallel", "parallel", "parallel"), vmem_limit_bytes=VMEM_LIMIT),
        name="diff",
    )(lam_params, p3, p3, p3, subln_g)


def _key_to_f32(key):
    bits = key ^ ((key >> 31) & jnp.int32(0x7FFFFFFF))
    return lax.bitcast_convert_type(bits, jnp.float32)


def _dsa_kernel(iq_ref, ik_ref, iw_ref, q_ref, k_ref, v_ref, o_ref,
                score_ref, wb_ref, acc_ref, j_ref, *, tq, n_top, idx_bits):
    t = tq
    qi = pl.program_id(1)
    n_chunks = qi + 1
    lo_mask, hi_mask = _lane_masks()
    halves = (lo_mask, hi_mask)
    causal = _causal_tile_mask(t)
    col = lax.broadcasted_iota(jnp.int32, (t, t), 1)

    iw = iw_ref[...]
    for h in range(IDX_HEADS):
        wb_ref[h] = jnp.broadcast_to(iw[:, h:h + 1], (t, LANES))

    def index_chunk(c, diag):
        off = pl.multiple_of(c * t, t)
        kk = ik_ref[pl.ds(off, t), :]
        sc = jnp.zeros((t, t), jnp.float32)
        for slab in range(IDX_HEADS // 2):
            iqs = iq_ref[:, slab * LANES:(slab + 1) * LANES]
            for hf in range(2):
                qh = jnp.where(halves[hf], iqs, jnp.zeros_like(iqs))
                lg = lax.dot_general(qh, kk, _NT, preferred_element_type=jnp.float32)
                w = wb_ref[2 * slab + hf]
                sc = sc + jnp.maximum(lg, 0.0) * jnp.concatenate([w] * (t // LANES), axis=1)
        if diag:
            sc = jnp.where(causal, sc, NEG_INF)
        score_ref[:, pl.ds(off, t)] = sc

    def index_body(c, carry):
        index_chunk(c, False)
        return carry

    lax.fori_loop(0, qi, index_body, 0)
    index_chunk(qi, True)

    def count_rows(hits):
        def body(c, acc):
            off = pl.multiple_of(c * t, t)
            hit = hits(score_ref[:, pl.ds(off, t)], c)
            for g in range(t // LANES):
                acc = acc + hit[:, g * LANES:(g + 1) * LANES]
            return acc
        acc = lax.fori_loop(0, n_chunks, body, jnp.zeros((t, LANES), jnp.float32))
        return jnp.sum(acc, axis=-1, keepdims=True)

    def count_ge(cand_key):
        cand = jnp.broadcast_to(_key_to_f32(cand_key), (t, t))
        return count_rows(lambda s, c: jnp.where(s >= cand, 1.0, 0.0))

    k_top = jnp.float32(n_top)
    zero_key = jnp.zeros((t, 1), jnp.int32)
    base = jnp.where(count_ge(zero_key) >= k_top, zero_key, jnp.int32(-2 ** 31))

    def bisect(it, base):
        cand_key = base | (jnp.int32(1) << (30 - it))
        return jnp.where(count_ge(cand_key) >= k_top, cand_key, base)

    base = lax.fori_loop(0, 31, bisect, base)
    lowest_key = jnp.int32(F32_LOWEST_BITS) ^ jnp.int32(0x7FFFFFFF)
    thr = _key_to_f32(jnp.maximum(base, lowest_key))
    thr_b = jnp.broadcast_to(thr, (t, t))

    c_gt = count_rows(lambda s, c: jnp.where(s > thr_b, 1.0, 0.0))
    c_eq = count_rows(lambda s, c: jnp.where(s == thr_b, 1.0, 0.0))
    need = k_top - c_gt
    j_ref[...] = jnp.full((t, LANES), 2 ** 30, jnp.int32)

    @pl.when(jnp.max(jnp.where(c_eq > need, 1.0, 0.0)) > 0.0)
    def _():
        def jbisect(it, lo):
            cand = lo + (jnp.int32(1) << (idx_bits - 1 - it))
            cand_b = jnp.broadcast_to(cand, (t, t))
            cnt = count_rows(lambda s, c: jnp.where(
                s == thr_b, jnp.where((col + c * t) <= cand_b, 1.0, 0.0), 0.0))
            return jnp.where(cnt < need, cand, lo)
        lo = lax.fori_loop(0, idx_bits, jbisect, jnp.full((t, 1), -1, jnp.int32))
        j_ref[...] = jnp.broadcast_to(lo + 1, (t, LANES))

    j_b = jnp.concatenate([j_ref[...]] * (t // LANES), axis=1)

    n_heads = 4
    for hd in range(n_heads):
        acc_ref[hd] = jnp.zeros((t, LANES), jnp.float32)
    qhs = []
    for slab in range(n_heads // 2):
        qs = q_ref[:, slab * LANES:(slab + 1) * LANES]
        for hf in range(2):
            qhs.append(jnp.where(halves[hf], qs, jnp.zeros_like(qs)))

    def attend(c, carry):
        off = pl.multiple_of(c * t, t)
        s_idx = score_ref[:, pl.ds(off, t)]
        keep_tie = jnp.where((col + c * t) <= j_b, 0.0, NEG_INF)
        bias = jnp.where(s_idx > thr_b, 0.0, jnp.where(s_idx == thr_b, keep_tie, NEG_INF))
        out = []
        for hd in range(n_heads):
            slab = hd // 2
            kj = k_ref[pl.ds(off, t), slab * LANES:(slab + 1) * LANES]
            vj = v_ref[pl.ds(off, t), slab * LANES:(slab + 1) * LANES]
            s = lax.dot_general(qhs[hd], kj, _NT, preferred_element_type=jnp.float32) + bias
            out.extend(_online_softmax_step(s, vj, carry[2 * hd], carry[2 * hd + 1], acc_ref, hd))
        return tuple(out)

    m0 = jnp.full((t, 1), NEG_INF, jnp.float32)
    l0 = jnp.zeros((t, 1), jnp.float32)
    fin = lax.fori_loop(0, n_chunks, attend, (m0, l0) * n_heads)
    for slab in range(n_heads // 2):
        o_lo = acc_ref[2 * slab] / fin[4 * slab + 1]
        o_hi = acc_ref[2 * slab + 1] / fin[4 * slab + 3]
        o_ref[:, slab * LANES:(slab + 1) * LANES] = _bf16(jnp.where(lo_mask, o_lo, o_hi))


def _dsa(p3, iw3, seq, tq):
    b = p3.shape[0]
    n_top = min(DSA_TOPK_MAX, seq // 4)
    idx_bits = max(1, (seq - 1).bit_length())
    return pl.pallas_call(
        functools.partial(_dsa_kernel, tq=tq, n_top=n_top, idx_bits=idx_bits),
        grid=(b, seq // tq),
        in_specs=[
            pl.BlockSpec((None, tq, 4 * LANES), lambda bi, qi: (bi, qi, SLAB_IQ // 4)),
            pl.BlockSpec((None, seq, LANES), lambda bi, qi: (bi, 0, SLAB_IK)),
            pl.BlockSpec((None, tq, LANES), lambda bi, qi: (bi, qi, 0)),
            pl.BlockSpec((None, tq, 2 * LANES), lambda bi, qi: (bi, qi, SLAB_DSA_Q // 2)),
            pl.BlockSpec((None, seq, 2 * LANES), lambda bi, qi: (bi, 0, SLAB_DSA_K // 2)),
            pl.BlockSpec((None, seq, 2 * LANES), lambda bi, qi: (bi, 0, SLAB_DSA_V // 2)),
        ],
        out_specs=pl.BlockSpec((None, tq, 2 * LANES), lambda bi, qi: (bi, qi, 0)),
        out_shape=jax.ShapeDtypeStruct((b, seq, 2 * LANES), ACT_DTYPE),
        scratch_shapes=[
            pltpu.VMEM((tq, seq), jnp.float32),
            pltpu.VMEM((IDX_HEADS, tq, LANES), jnp.float32),
            pltpu.VMEM((4, tq, LANES), jnp.float32),
            pltpu.VMEM((tq, LANES), jnp.int32),
        ],
        compiler_params=pltpu.CompilerParams(
            dimension_semantics=("parallel", "parallel"), vmem_limit_bytes=VMEM_LIMIT),
        name="dsa",
    )(p3, p3, iw3, p3, p3, p3)


def _ffn_kernel(x_ref, moba_ref, diff_ref, dsa_ref, wo_ref, g_ref, wg_ref, wu_ref, wd_ref, o_ref, *, ff_chunk):
    w_moba, w_diff = 2 * LANES, 4 * LANES
    y = (x_ref[...]
         + jnp.dot(moba_ref[...], wo_ref[0:w_moba, :], preferred_element_type=jnp.float32)
         + jnp.dot(diff_ref[...], wo_ref[w_moba:w_moba + w_diff, :], preferred_element_type=jnp.float32)
         + jnp.dot(dsa_ref[...], wo_ref[w_moba + w_diff:, :], preferred_element_type=jnp.float32))
    ms = jnp.mean(y * y, axis=-1, keepdims=True)
    h = _bf16(y * lax.rsqrt(ms + EPS) * g_ref[...])
    o_ref[...] = y
    for c in range(D_FF // ff_chunk):
        sl = slice(c * ff_chunk, (c + 1) * ff_chunk)
        gate = jnp.dot(h, wg_ref[:, sl], preferred_element_type=jnp.float32)
        up = jnp.dot(h, wu_ref[:, sl], preferred_element_type=jnp.float32)
        act = _bf16(gate * (1.0 / (1.0 + jnp.exp(-gate))) * up)
        o_ref[...] += jnp.dot(act, wd_ref[sl, :], preferred_element_type=jnp.float32)


def _ffn(x2d, moba, diff, dsa, wo, g, wg, wu, wd, tm, ff_chunk):
    m = x2d.shape[0]
    const = lambda i: (0, 0)
    resident = dict(pipeline_mode=pl.Buffered(1))
    return pl.pallas_call(
        functools.partial(_ffn_kernel, ff_chunk=ff_chunk),
        grid=(m // tm,),
        in_specs=[
            pl.BlockSpec((tm, D_MODEL), lambda i: (i, 0)),
            pl.BlockSpec((tm, 2 * LANES), lambda i: (i, 0)),
            pl.BlockSpec((tm, 4 * LANES), lambda i: (i, 0)),
            pl.BlockSpec((tm, 2 * LANES), lambda i: (i, 0)),
            pl.BlockSpec((D_MODEL, D_MODEL), const, **resident),
            pl.BlockSpec((1, D_MODEL), const),
            pl.BlockSpec((D_MODEL, D_FF), const, **resident),
            pl.BlockSpec((D_MODEL, D_FF), const, **resident),
            pl.BlockSpec((D_FF, D_MODEL), const, **resident),
        ],
        out_specs=pl.BlockSpec((tm, D_MODEL), lambda i: (i, 0)),
        out_shape=jax.ShapeDtypeStruct((m, D_MODEL), jnp.float32),
        compiler_params=pltpu.CompilerParams(
            dimension_semantics=("parallel",), vmem_limit_bytes=VMEM_LIMIT),
        name="outproj_ffn",
    )(x2d, moba, diff, dsa, wo, g, wg, wu, wd)


def _rope_tables(seq):
    inv = 1.0 / (ROPE_THETA ** (jnp.arange(0, HEAD_DIM, 2, dtype=jnp.float32) / HEAD_DIM))
    ang = jnp.arange(seq, dtype=jnp.float32)[:, None] * inv[None, :]
    cos, sin = jnp.cos(ang), jnp.sin(ang)
    return jnp.tile(cos, (1, 4)), jnp.concatenate([-sin, sin, -sin, sin], axis=1)


def _pack_w_in(w):
    ik = w[:, SLAB_IK * LANES:SLAB_IK * LANES + HEAD_DIM]
    iw = w[:, SLAB_IK * LANES + HEAD_DIM:IN_COLS]
    pad = jnp.zeros((w.shape[0], LANES - IDX_HEADS), w.dtype)
    return _bf16(jnp.concatenate([w[:, :SLAB_IK * LANES], ik, ik, iw, pad], axis=1))


def kernel(x, attn_norm_g, w_in, q_norm_g, k_norm_g, diff_lambda, diff_subln_g,
           w_out, ffn_norm_g, w_gate, w_up, w_down):
    b, seq, d = x.shape
    depth = w_in.shape[0]
    assert d == D_MODEL and seq % MOBA_BLOCK == 0
    tq = MOBA_BLOCK
    tm_in = 256
    tm_ffn = 256
    cos_t, sin_t = _rope_tables(seq)
    x2d = x.reshape(b * seq, d)

    for l in range(depth):
        gains = jnp.stack([q_norm_g[l, 0], k_norm_g[l, 0], q_norm_g[l, 1], k_norm_g[l, 1],
                           q_norm_g[l, 2], k_norm_g[l, 2]])
        head_gains = jnp.concatenate(
            [jnp.tile(gains, (1, 2)), jnp.ones((2, LANES), jnp.float32)], axis=0)
        p, iw = _inproj(x2d, attn_norm_g[l][None, :], _pack_w_in(w_in[l]), cos_t, sin_t,
                        head_gains, seq, tm_in)
        p3 = p.reshape(b, seq, N_SLABS_BF16 * LANES)
        iw3 = iw.reshape(b, seq, LANES)

        moba = _moba(p3, seq)
        lam_init = 0.8 - 0.6 * math.exp(-0.3 * l)
        diff = _diff(p3, diff_lambda[l], diff_subln_g[l][None, :], seq, lam_init, tq)
        dsa = _dsa(p3, iw3, seq, tq)

        x2d = _ffn(x2d, moba.reshape(b * seq, -1), diff.reshape(b * seq, -1), dsa.reshape(b * seq, -1),
                   _bf16(w_out[l]), ffn_norm_g[l][None, :], _bf16(w_gate[l]), _bf16(w_up[l]),
                   _bf16(w_down[l]), tm_ffn, 704)
    return x2d.reshape(b, seq, d)
```

```python
import functools
import math

import jax
import jax.numpy as jnp
from jax import lax
from jax.experimental import pallas as pl
from jax.experimental.pallas import tpu as pltpu

D_MODEL = 1024
HEAD_DIM = 64
MOBA_BLOCK = 256
MOBA_TOPK = 3
DSA_TOPK_MAX = 256
IDX_HEADS = 8
ROPE_THETA = 10000.0
EPS = 1e-6
D_FF = 2816

LANES = 128
HALF = HEAD_DIM // 2

SLAB_MOBA_Q, SLAB_MOBA_K, SLAB_MOBA_V = 0, 2, 4
SLAB_DIFF_Q, SLAB_DIFF_K, SLAB_DIFF_V = 6, 10, 14
SLAB_DSA_Q, SLAB_DSA_K, SLAB_DSA_V = 18, 20, 22
SLAB_IQ, SLAB_IK, SLAB_IW = 24, 28, 29
N_SLABS_BF16 = 29
N_SLABS = 30
IN_COLS = 3656

ATTN_SCALE = HEAD_DIM ** -0.5 * math.log2(math.e)
IDX_SCALE = 64 ** -0.5

VMEM_LIMIT = 56 * 1024 * 1024
ACT_DTYPE = jnp.bfloat16

NEG_INF = float("-inf")
F32_LOWEST_BITS = -8388609

_NT = (((1,), (1,)), ((), ()))


def _f32(x):
    return x.astype(jnp.float32)


def _bf16(x):
    return x.astype(ACT_DTYPE)


def _slab_kind(s):
    if s < SLAB_MOBA_K:
        return True, 0, True, ATTN_SCALE
    if s < SLAB_MOBA_V:
        return True, 1, True, 1.0
    if s < SLAB_DIFF_Q:
        return False, 0, False, 1.0
    if s < SLAB_DIFF_K:
        return True, 2, True, ATTN_SCALE
    if s < SLAB_DIFF_V:
        return True, 3, True, 1.0
    if s < SLAB_DSA_Q:
        return False, 0, False, 1.0
    if s < SLAB_DSA_K:
        return True, 4, True, ATTN_SCALE
    if s < SLAB_DSA_V:
        return True, 5, True, 1.0
    if s < SLAB_IQ:
        return False, 0, False, 1.0
    if s < SLAB_IK:
        return False, 0, True, IDX_SCALE
    if s < SLAB_IW:
        return False, 0, True, 1.0
    return False, 0, False, IDX_HEADS ** -0.5


def _inproj_kernel(x_ref, g_ref, w_ref, cos_ref, sin_ref, hg_ref, p_ref, iw_ref):
    x = x_ref[...]
    ms = jnp.mean(x * x, axis=-1, keepdims=True)
    h = _bf16(x * lax.rsqrt(ms + EPS) * g_ref[...])

    lane = lax.broadcasted_iota(jnp.int32, (1, LANES), 1)
    first_half = (lane & (HEAD_DIM - 1)) < HALF
    r = lax.broadcasted_iota(jnp.int32, (LANES, LANES), 0) // HEAD_DIM
    c = lax.broadcasted_iota(jnp.int32, (LANES, LANES), 1) // HEAD_DIM
    head_ones = _bf16(jnp.where(r == c, 1.0, 0.0))
    cos = cos_ref[...]
    sin = sin_ref[...]
    hg = hg_ref[...]

    for pair in range(N_SLABS // 2):
        y2 = jnp.dot(h, w_ref[:, pair * 2 * LANES:(pair + 1) * 2 * LANES],
                     preferred_element_type=jnp.float32)
        for half in range(2):
            s = pair * 2 + half
            y = y2[:, half * LANES:(half + 1) * LANES]
            norm, grow, rope, scale = _slab_kind(s)
            if norm:
                yy = y * y
                hi = _bf16(yy)
                lo = _bf16(yy - _f32(hi))
                ss = (jnp.dot(hi, head_ones, preferred_element_type=jnp.float32)
                      + jnp.dot(lo, head_ones, preferred_element_type=jnp.float32))
                y = y * lax.rsqrt(ss * (1.0 / HEAD_DIM) + EPS) * hg[grow:grow + 1, :]
            if rope:
                partner = jnp.where(first_half, pltpu.roll(y, LANES - HALF, 1), pltpu.roll(y, HALF, 1))
                y = y * cos + partner * sin
            if scale != 1.0:
                y = y * scale
            if s < N_SLABS_BF16:
                p_ref[:, s * LANES:(s + 1) * LANES] = _bf16(y)
            else:
                iw_ref[...] = y


def _inproj(x2d, g, w_packed, cos_t, sin_t, head_gains, seq, tm):
    m = x2d.shape[0]
    nc = w_packed.shape[1]
    n_seq_tiles = seq // tm
    return pl.pallas_call(
        _inproj_kernel,
        grid=(m // tm,),
        in_specs=[
            pl.BlockSpec((tm, D_MODEL), lambda i: (i, 0)),
            pl.BlockSpec((1, D_MODEL), lambda i: (0, 0)),
            pl.BlockSpec((D_MODEL, nc), lambda i: (0, 0)),
            pl.BlockSpec((tm, LANES), lambda i: (i % n_seq_tiles, 0)),
            pl.BlockSpec((tm, LANES), lambda i: (i % n_seq_tiles, 0)),
            pl.BlockSpec((8, LANES), lambda i: (0, 0)),
        ],
        out_specs=[
            pl.BlockSpec((tm, N_SLABS_BF16 * LANES), lambda i: (i, 0)),
            pl.BlockSpec((tm, LANES), lambda i: (i, 0)),
        ],
        out_shape=[
            jax.ShapeDtypeStruct((m, N_SLABS_BF16 * LANES), ACT_DTYPE),
            jax.ShapeDtypeStruct((m, LANES), jnp.float32),
        ],
        compiler_params=pltpu.CompilerParams(
            dimension_semantics=("parallel",), vmem_limit_bytes=VMEM_LIMIT),
        name="inproj",
    )(x2d, g, w_packed, cos_t, sin_t, head_gains)


def _attend_pipelined(n_tiles, n_maps, t, score_tile, query_keep, value_tile, s_ref, p_ref, acc_ref, sum_p):
    maps = range(n_maps)

    def put_scores(j, slot):
        smax = []
        for c in maps:
            s = score_tile(j, c)
            s_ref[slot, c] = s
            smax.append(jnp.where(query_keep(j, c), jnp.max(s, axis=0, keepdims=True), NEG_INF))
        return tuple(smax)

    def softmax(j, slot, smax, m_old, l_old):
        alpha, m_new, l_new = [], [], []
        for c in maps:
            m = jnp.maximum(m_old[c], smax[c])
            m_safe = jnp.where(m == NEG_INF, 0.0, m)
            a = jnp.exp2(m_old[c] - m_safe)
            p = jnp.exp2(s_ref[slot, c] - jnp.where(query_keep(j, c), m_safe, jnp.inf))
            p_ref[slot, c] = _bf16(p)
            alpha.append(a)
            m_new.append(m)
            l_new.append(a * l_old[c] + jnp.sum(p, axis=0, keepdims=True) if sum_p else l_old[c])
        return tuple(alpha), tuple(m_new), tuple(l_new)

    def add_values(j, slot, alpha):
        for c in maps:
            acc_ref[c] = alpha[c] * acc_ref[c] + jnp.dot(value_tile(j, c), p_ref[slot, c],
                                                         preferred_element_type=jnp.float32)

    def trip(i, carry):
        smax0, alpha_prev, m, l = carry
        j = 2 * i
        smax1 = put_scores(j + 1, 1)
        alpha0, m, l = softmax(j, 0, smax0, m, l)
        add_values(j - 1, 1, alpha_prev)
        smax0 = put_scores(j + 2, 0)
        alpha1, m, l = softmax(j + 1, 1, smax1, m, l)
        add_values(j, 0, alpha0)
        return smax0, alpha1, m, l

    m0 = (jnp.full((1, t), NEG_INF, jnp.float32),) * n_maps
    l0 = (jnp.zeros((1, t), jnp.float32),) * n_maps
    one = (jnp.ones((1, t), jnp.float32),) * n_maps
    for c in maps:
        p_ref[1, c] = jnp.zeros((t, t), p_ref.dtype)
    n_trips = (n_tiles + 1) // 2
    _, alpha_last, _, l_fin = lax.fori_loop(0, n_trips, trip, (put_scores(0, 0), one, m0, l0))
    add_values(2 * n_trips - 1, 1, alpha_last)
    return l_fin


def _causal_bias_tiles(t):
    key = lax.broadcasted_iota(jnp.int32, (t, t), 0)
    qry = lax.broadcasted_iota(jnp.int32, (t, t), 1)
    tri = jnp.where(key <= qry, 0.0, NEG_INF).astype(jnp.float32)
    return jnp.stack([jnp.zeros((t, t), jnp.float32), tri])


def _lane_masks():
    lane = lax.broadcasted_iota(jnp.int32, (1, LANES), 1)
    return lane < HEAD_DIM, lane >= HEAD_DIM


def _split_heads_in_slab(q):
    lo_mask, hi_mask = _lane_masks()
    zero = jnp.zeros_like(q)
    return jnp.where(lo_mask, q, zero), jnp.where(hi_mask, q, zero)


def _causal_tile_mask_t(t):
    key = lax.broadcasted_iota(jnp.int32, (t, t), 0)
    qry = lax.broadcasted_iota(jnp.int32, (t, t), 1)
    return key <= qry


def _v_tiles(p3, slab0, n_slabs, tk):
    b, s, _ = p3.shape
    v = p3[:, :, slab0 * LANES:(slab0 + n_slabs) * LANES].reshape(b, s // tk, tk, n_slabs, LANES)
    return v.transpose(0, 3, 1, 4, 2)


def _v_tiles_with_ones(p3, slab0, n_heads, tk):
    b, s, _ = p3.shape
    v = p3[:, :, slab0 * LANES:slab0 * LANES + n_heads * HEAD_DIM].reshape(b, s // tk, tk, n_heads, HEAD_DIM)
    v = v.transpose(0, 3, 1, 4, 2)
    return jnp.concatenate([v, jnp.ones_like(v)], axis=3)


def _moba_kernel(q_ref, k_ref, vt_ref, bias_ref, o_ref, kmean_ref, sel_ref, acc_ref, s_ref, p_ref, *, n_blocks):
    t = MOBA_BLOCK
    qi = pl.program_id(2)

    @pl.when(qi == 0)
    def _():
        kmean_ref[...] = jnp.zeros_like(kmean_ref)
        for j in range(n_blocks):
            kb = _f32(k_ref[j * t:(j + 1) * t, :])
            kmean_ref[j:j + 1, :] = jnp.sum(kb, axis=0, keepdims=True) * (1.0 / t)

    km = kmean_ref[...]
    km_hi = _bf16(km)
    km_lo = _bf16(km - _f32(km_hi))
    blk = lax.broadcasted_iota(jnp.int32, (LANES, t), 0)
    blk_f = _f32(blk)
    n_sel = min(MOBA_TOPK, n_blocks - 1)
    qhs = _split_heads_in_slab(q_ref[...])

    for hd, qh in enumerate(qhs):
        gate = (lax.dot_general(km_hi, qh, _NT, preferred_element_type=jnp.float32)
                + lax.dot_general(km_lo, qh, _NT, preferred_element_type=jnp.float32))
        gate = jnp.where(blk < qi, gate, NEG_INF)
        sel = jnp.zeros((LANES, t), jnp.float32)
        for rank in range(n_sel):
            gmax = jnp.max(gate, axis=0, keepdims=True)
            first = jnp.min(jnp.where(gate == gmax, blk_f, float(LANES)), axis=0, keepdims=True)
            pick = blk_f == first
            sel = jnp.where(pick, jnp.maximum(sel, jnp.where(rank < qi, 1.0, 0.0)), sel)
            gate = jnp.where(pick, NEG_INF, gate)
        sel_ref[hd] = jnp.where(blk == qi, 1.0, sel)
        acc_ref[hd] = jnp.zeros((LANES, t), jnp.float32)

    def score_tile(j, c):
        kj = k_ref[pl.ds(pl.multiple_of(jnp.minimum(j, qi) * t, t), t), :]
        s = lax.dot_general(kj, qhs[c], _NT, preferred_element_type=jnp.float32)
        return s + bias_ref[jnp.where(j >= qi, 1, 0)]

    def query_keep(j, c):
        return sel_ref[c, pl.ds(jnp.minimum(j, LANES - 1), 1), :] > 0.0

    def value_tile(j, c):
        return vt_ref[c, jnp.clip(j, 0, qi)]

    _attend_pipelined(qi + 1, 2, t, score_tile, query_keep, value_tile, s_ref, p_ref, acc_ref, False)
    outs = [acc_ref[c, 0:HEAD_DIM, :] / acc_ref[c, HEAD_DIM:HEAD_DIM + 1, :] for c in range(2)]
    o_ref[...] = _bf16(jnp.concatenate(outs, axis=0).T)


def _moba(p3, seq):
    b = p3.shape[0]
    t = MOBA_BLOCK
    n_blocks = seq // t
    vt = _v_tiles_with_ones(p3, SLAB_MOBA_V, 4, t)
    return pl.pallas_call(
        functools.partial(_moba_kernel, n_blocks=n_blocks),
        grid=(b, 2, n_blocks),
        in_specs=[
            pl.BlockSpec((None, t, LANES), lambda bi, s, qi: (bi, qi, SLAB_MOBA_Q + s)),
            pl.BlockSpec((None, seq, LANES), lambda bi, s, qi: (bi, 0, SLAB_MOBA_K + s)),
            pl.BlockSpec((None, 2, n_blocks, LANES, t), lambda bi, s, qi: (bi, s, 0, 0, 0)),
            pl.BlockSpec((2, t, t), lambda bi, s, qi: (0, 0, 0)),
        ],
        out_specs=pl.BlockSpec((None, t, LANES), lambda bi, s, qi: (bi, qi, s)),
        out_shape=jax.ShapeDtypeStruct((b, seq, 2 * LANES), ACT_DTYPE),
        scratch_shapes=[
            pltpu.VMEM((LANES, LANES), jnp.float32),
            pltpu.VMEM((2, LANES, t), jnp.float32),
            pltpu.VMEM((2, LANES, t), jnp.float32),
            pltpu.VMEM((2, 2, t, t), jnp.float32),
            pltpu.VMEM((2, 2, t, t), ACT_DTYPE),
        ],
        compiler_params=pltpu.CompilerParams(
            dimension_semantics=("parallel", "parallel", "arbitrary"), vmem_limit_bytes=VMEM_LIMIT),
        name="moba",
    )(p3, p3, vt, _causal_bias_tiles(t))


def _diff_kernel(lam_ref, q_ref, k_ref, vt_ref, sg_ref, bias_ref, o_ref, acc_ref, s_ref, p_ref, *, t, lam_init):
    qi = pl.program_id(2)
    qs = _split_heads_in_slab(q_ref[...])

    def score_tile(j, c):
        kj = k_ref[pl.ds(pl.multiple_of(jnp.minimum(j, qi) * t, t), t), :]
        s = lax.dot_general(kj, qs[c], _NT, preferred_element_type=jnp.float32)
        return s + bias_ref[jnp.where(j >= qi, 1, 0)]

    def query_keep(j, c):
        return jnp.broadcast_to(j <= qi, (1, t))

    def value_tile(j, c):
        return vt_ref[jnp.clip(j, 0, qi)]

    for c in range(2):
        acc_ref[c] = jnp.zeros((LANES, t), jnp.float32)
    l1, l2 = _attend_pipelined(qi + 1, 2, t, score_tile, query_keep, value_tile, s_ref, p_ref, acc_ref, True)

    lp = lam_ref[...]
    lam = (jnp.exp(jnp.sum(lp[0:1, :] * lp[1:2, :], axis=-1, keepdims=True))
           - jnp.exp(jnp.sum(lp[2:3, :] * lp[3:4, :], axis=-1, keepdims=True)) + lam_init)
    a = (acc_ref[0] / l1 - lam * (acc_ref[1] / l2)).T
    ms = jnp.mean(a * a, axis=-1, keepdims=True)
    o_ref[...] = _bf16(a * lax.rsqrt(ms + EPS) * sg_ref[...] * (1.0 - lam_init))


def _diff(p3, lam_params, subln_g, seq, lam_init, t):
    b = p3.shape[0]
    n_heads = 4
    vt = _v_tiles(p3, SLAB_DIFF_V, n_heads, t)
    return pl.pallas_call(
        functools.partial(_diff_kernel, t=t, lam_init=lam_init),
        grid=(b, n_heads, seq // t),
        in_specs=[
            pl.BlockSpec((4, HEAD_DIM), lambda bi, h, qi: (0, 0)),
            pl.BlockSpec((None, t, LANES), lambda bi, h, qi: (bi, qi, SLAB_DIFF_Q + h)),
            pl.BlockSpec((None, seq, LANES), lambda bi, h, qi: (bi, 0, SLAB_DIFF_K + h)),
            pl.BlockSpec((None, None, seq // t, LANES, t), lambda bi, h, qi: (bi, h, 0, 0, 0)),
            pl.BlockSpec((1, LANES), lambda bi, h, qi: (0, 0)),
            pl.BlockSpec((2, t, t), lambda bi, h, qi: (0, 0, 0)),
        ],
        out_specs=pl.BlockSpec((None, t, LANES), lambda bi, h, qi: (bi, qi, h)),
        out_shape=jax.ShapeDtypeStruct((b, seq, n_heads * LANES), ACT_DTYPE),
        scratch_shapes=[pltpu.VMEM((2, LANES, t), jnp.float32),
                        pltpu.VMEM((2, 2, t, t), jnp.float32),
                        pltpu.VMEM((2, 2, t, t), ACT_DTYPE)],
        compiler_params=pltpu.CompilerParams(
            dimension_semantics=("parallel", "parallel", "parallel"), vmem_limit_bytes=VMEM_LIMIT),
        name="diff",
    )(lam_params, p3, p3, vt, subln_g, _causal_bias_tiles(t))


def _key_to_f32(key):
    bits = key ^ ((key >> 31) & jnp.int32(0x7FFFFFFF))
    return lax.bitcast_convert_type(bits, jnp.float32)


def _dsa_kernel(iq_ref, ik_ref, iw_ref, q_ref, k_ref, vt_ref, o_ref,
                score_ref, acc_ref, j_ref, s_ref, p_ref, *, t, n_top, idx_bits):
    qi = pl.program_id(1)
    n_chunks = qi + 1
    causal = _causal_tile_mask_t(t)
    key_in_tile = lax.broadcasted_iota(jnp.int32, (t, t), 0)

    iw_t = iw_ref[...].T
    iq_heads = []
    for slab in range(IDX_HEADS // 2):
        iq_heads.extend(_split_heads_in_slab(iq_ref[:, slab * LANES:(slab + 1) * LANES]))

    def index_chunk(c, diag):
        kk = ik_ref[pl.ds(pl.multiple_of(c * t, t), t), :]
        sc = jnp.zeros((t, t), jnp.float32)
        for h in range(IDX_HEADS):
            lg = lax.dot_general(kk, iq_heads[h], _NT, preferred_element_type=jnp.float32)
            sc = sc + jnp.maximum(lg, 0.0) * iw_t[h:h + 1, :]
        if diag:
            sc = jnp.where(causal, sc, NEG_INF)
        score_ref[c] = sc

    def index_body(c, carry):
        index_chunk(c, False)
        return carry

    lax.fori_loop(0, qi, index_body, 0)
    index_chunk(qi, True)

    def count_keys(hits):
        def body(c, acc):
            return acc + jnp.sum(hits(score_ref[c], c).reshape(t // 8, 8, t), axis=0)
        acc = lax.fori_loop(0, n_chunks, body, jnp.zeros((8, t), jnp.float32))
        return jnp.sum(acc, axis=0, keepdims=True)

    def count_ge(cand_key):
        cand = _key_to_f32(cand_key)
        return count_keys(lambda s, c: jnp.where(s >= cand, 1.0, 0.0))

    k_top = jnp.float32(n_top)
    zero_key = jnp.zeros((1, t), jnp.int32)
    base = jnp.where(count_ge(zero_key) >= k_top, zero_key, jnp.int32(-2 ** 31))

    def bisect(it, base):
        cand_key = base | (jnp.int32(1) << (30 - it))
        return jnp.where(count_ge(cand_key) >= k_top, cand_key, base)

    base = lax.fori_loop(0, 31, bisect, base)
    lowest_key = jnp.int32(F32_LOWEST_BITS) ^ jnp.int32(0x7FFFFFFF)
    thr = _key_to_f32(jnp.maximum(base, lowest_key))

    c_gt = count_keys(lambda s, c: jnp.where(s > thr, 1.0, 0.0))
    c_eq = count_keys(lambda s, c: jnp.where(s == thr, 1.0, 0.0))
    need = k_top - c_gt
    j_ref[...] = jnp.full(j_ref.shape, 2 ** 30, jnp.int32)

    @pl.when(jnp.max(jnp.where(c_eq > need, 1.0, 0.0)) > 0.0)
    def _():
        def jbisect(it, lo):
            cand = lo + (jnp.int32(1) << (idx_bits - 1 - it))
            cnt = count_keys(lambda s, c: jnp.where(
                s == thr, jnp.where((key_in_tile + c * t) <= cand, 1.0, 0.0), 0.0))
            return jnp.where(cnt < need, cand, lo)
        lo = lax.fori_loop(0, idx_bits, jbisect, jnp.full((1, t), -1, jnp.int32))
        j_ref[...] = jnp.broadcast_to(lo + 1, j_ref.shape)

    j_last = j_ref[0:1, :]

    n_heads = 4
    qhs = []
    for slab in range(n_heads // 2):
        qhs.extend(_split_heads_in_slab(q_ref[:, slab * LANES:(slab + 1) * LANES]))
    for hd in range(n_heads):
        acc_ref[hd] = jnp.zeros((LANES, t), jnp.float32)

    shared = {}

    def score_tile(j, hd):
        jc = jnp.minimum(j, qi)
        if hd == 0:
            s_idx = score_ref[jc]
            keep_tie = jnp.where((key_in_tile + jc * t) <= j_last, 0.0, NEG_INF)
            shared["bias"] = jnp.where(s_idx > thr, 0.0, jnp.where(s_idx == thr, keep_tie, NEG_INF))
        slab = hd // 2
        kj = k_ref[pl.ds(pl.multiple_of(jc * t, t), t), slab * LANES:(slab + 1) * LANES]
        return lax.dot_general(kj, qhs[hd], _NT, preferred_element_type=jnp.float32) + shared["bias"]

    def query_keep(j, hd):
        return jnp.broadcast_to(j <= qi, (1, t))

    def value_tile(j, hd):
        return vt_ref[hd, jnp.clip(j, 0, qi)]

    _attend_pipelined(n_chunks, n_heads, t, score_tile, query_keep, value_tile, s_ref, p_ref, acc_ref, False)
    outs = [acc_ref[hd, 0:HEAD_DIM, :] / acc_ref[hd, HEAD_DIM:HEAD_DIM + 1, :] for hd in range(n_heads)]
    for slab in range(n_heads // 2):
        o_ref[:, slab * LANES:(slab + 1) * LANES] = _bf16(jnp.concatenate(outs[2 * slab:2 * slab + 2], axis=0).T)


def _dsa(p3, iw3, seq, t):
    b = p3.shape[0]
    n_top = min(DSA_TOPK_MAX, seq // 4)
    idx_bits = max(1, (seq - 1).bit_length())
    n_tiles = seq // t
    vt = _v_tiles_with_ones(p3, SLAB_DSA_V, 4, t)
    return pl.pallas_call(
        functools.partial(_dsa_kernel, t=t, n_top=n_top, idx_bits=idx_bits),
        grid=(b, n_tiles),
        in_specs=[
            pl.BlockSpec((None, t, 4 * LANES), lambda bi, qi: (bi, qi, SLAB_IQ // 4)),
            pl.BlockSpec((None, seq, LANES), lambda bi, qi: (bi, 0, SLAB_IK)),
            pl.BlockSpec((None, t, LANES), lambda bi, qi: (bi, qi, 0)),
            pl.BlockSpec((None, t, 2 * LANES), lambda bi, qi: (bi, qi, SLAB_DSA_Q // 2)),
            pl.BlockSpec((None, seq, 2 * LANES), lambda bi, qi: (bi, 0, SLAB_DSA_K // 2)),
            pl.BlockSpec((None, 4, n_tiles, LANES, t), lambda bi, qi: (bi, 0, 0, 0, 0)),
        ],
        out_specs=pl.BlockSpec((None, t, 2 * LANES), lambda bi, qi: (bi, qi, 0)),
        out_shape=jax.ShapeDtypeStruct((b, seq, 2 * LANES), ACT_DTYPE),
        scratch_shapes=[
            pltpu.VMEM((n_tiles, t, t), jnp.float32),
            pltpu.VMEM((4, LANES, t), jnp.float32),
            pltpu.VMEM((8, t), jnp.int32),
            pltpu.VMEM((2, 4, t, t), jnp.float32),
            pltpu.VMEM((2, 4, t, t), ACT_DTYPE),
        ],
        compiler_params=pltpu.CompilerParams(
            dimension_semantics=("parallel", "parallel"), vmem_limit_bytes=VMEM_LIMIT),
        name="dsa",
    )(p3, p3, iw3, p3, p3, vt)


def _ffn_kernel(x_ref, moba_ref, diff_ref, dsa_ref, wo_ref, g_ref, wg_ref, wu_ref, wd_ref, o_ref, *, ff_chunk):
    w_moba, w_diff = 2 * LANES, 4 * LANES
    y = (x_ref[...]
         + jnp.dot(moba_ref[...], wo_ref[0:w_moba, :], preferred_element_type=jnp.float32)
         + jnp.dot(diff_ref[...], wo_ref[w_moba:w_moba + w_diff, :], preferred_element_type=jnp.float32)
         + jnp.dot(dsa_ref[...], wo_ref[w_moba + w_diff:, :], preferred_element_type=jnp.float32))
    ms = jnp.mean(y * y, axis=-1, keepdims=True)
    h = _bf16(y * lax.rsqrt(ms + EPS) * g_ref[...])
    o_ref[...] = y
    for c in range(D_FF // ff_chunk):
        sl = slice(c * ff_chunk, (c + 1) * ff_chunk)
        gate = jnp.dot(h, wg_ref[:, sl], preferred_element_type=jnp.float32)
        up = jnp.dot(h, wu_ref[:, sl], preferred_element_type=jnp.float32)
        act = _bf16(gate * (1.0 / (1.0 + jnp.exp(-gate))) * up)
        o_ref[...] += jnp.dot(act, wd_ref[sl, :], preferred_element_type=jnp.float32)


def _ffn(x2d, moba, diff, dsa, wo, g, wg, wu, wd, tm, ff_chunk):
    m = x2d.shape[0]
    const = lambda i: (0, 0)
    resident = dict(pipeline_mode=pl.Buffered(1))
    return pl.pallas_call(
        functools.partial(_ffn_kernel, ff_chunk=ff_chunk),
        grid=(m // tm,),
        in_specs=[
            pl.BlockSpec((tm, D_MODEL), lambda i: (i, 0)),
            pl.BlockSpec((tm, 2 * LANES), lambda i: (i, 0)),
            pl.BlockSpec((tm, 4 * LANES), lambda i: (i, 0)),
            pl.BlockSpec((tm, 2 * LANES), lambda i: (i, 0)),
            pl.BlockSpec((D_MODEL, D_MODEL), const, **resident),
            pl.BlockSpec((1, D_MODEL), const),
            pl.BlockSpec((D_MODEL, D_FF), const, **resident),
            pl.BlockSpec((D_MODEL, D_FF), const, **resident),
            pl.BlockSpec((D_FF, D_MODEL), const, **resident),
        ],
        out_specs=pl.BlockSpec((tm, D_MODEL), lambda i: (i, 0)),
        out_shape=jax.ShapeDtypeStruct((m, D_MODEL), jnp.float32),
        compiler_params=pltpu.CompilerParams(
            dimension_semantics=("parallel",), vmem_limit_bytes=VMEM_LIMIT),
        name="outproj_ffn",
    )(x2d, moba, diff, dsa, wo, g, wg, wu, wd)


def _rope_tables(seq):
    inv = 1.0 / (ROPE_THETA ** (jnp.arange(0, HEAD_DIM, 2, dtype=jnp.float32) / HEAD_DIM))
    ang = jnp.arange(seq, dtype=jnp.float32)[:, None] * inv[None, :]
    cos, sin = jnp.cos(ang), jnp.sin(ang)
    return jnp.tile(cos, (1, 4)), jnp.concatenate([-sin, sin, -sin, sin], axis=1)


def _pack_w_in(w):
    ik = w[:, SLAB_IK * LANES:SLAB_IK * LANES + HEAD_DIM]
    iw = w[:, SLAB_IK * LANES + HEAD_DIM:IN_COLS]
    pad = jnp.zeros((w.shape[0], LANES - IDX_HEADS), w.dtype)
    return _bf16(jnp.concatenate([w[:, :SLAB_IK * LANES], ik, ik, iw, pad], axis=1))


def kernel(x, attn_norm_g, w_in, q_norm_g, k_norm_g, diff_lambda, diff_subln_g,
           w_out, ffn_norm_g, w_gate, w_up, w_down):
    b, seq, d = x.shape
    depth = w_in.shape[0]
    assert d == D_MODEL and seq % MOBA_BLOCK == 0
    tq = MOBA_BLOCK
    tm_in = 256
    tm_ffn = 256
    cos_t, sin_t = _rope_tables(seq)
    x2d = x.reshape(b * seq, d)

    for l in range(depth):
        gains = jnp.stack([q_norm_g[l, 0], k_norm_g[l, 0], q_norm_g[l, 1], k_norm_g[l, 1],
                           q_norm_g[l, 2], k_norm_g[l, 2]])
        head_gains = jnp.concatenate(
            [jnp.tile(gains, (1, 2)), jnp.ones((2, LANES), jnp.float32)], axis=0)
        p, iw = _inproj(x2d, attn_norm_g[l][None, :], _pack_w_in(w_in[l]), cos_t, sin_t,
                        head_gains, seq, tm_in)
        p3 = p.reshape(b, seq, N_SLABS_BF16 * LANES)
        iw3 = iw.reshape(b, seq, LANES)

        moba = _moba(p3, seq)
        lam_init = 0.8 - 0.6 * math.exp(-0.3 * l)
        diff = _diff(p3, diff_lambda[l], diff_subln_g[l][None, :], seq, lam_init, tq)
        dsa = _dsa(p3, iw3, seq, tq)

        x2d = _ffn(x2d, moba.reshape(b * seq, -1), diff.reshape(b * seq, -1), dsa.reshape(b * seq, -1),
                   _bf16(w_out[l]), ffn_norm_g[l][None, :], _bf16(w_gate[l]), _bf16(w_up[l]),
                   _bf16(w_down[l]), tm_ffn, 704)
    return x2d.reshape(b, seq, d)
```

```python
import functools
import math

import jax
import jax.numpy as jnp
from jax import lax
from jax.experimental import pallas as pl
from jax.experimental.pallas import tpu as pltpu

D_MODEL = 1024
HEAD_DIM = 64
MOBA_BLOCK = 256
MOBA_TOPK = 3
DSA_TOPK_MAX = 256
IDX_HEADS = 8
ROPE_THETA = 10000.0
EPS = 1e-6
D_FF = 2816

LANES = 128
HALF = HEAD_DIM // 2

SLAB_MOBA_Q, SLAB_MOBA_K, SLAB_MOBA_V = 0, 2, 4
SLAB_DIFF_Q, SLAB_DIFF_K, SLAB_DIFF_V = 6, 10, 14
SLAB_DSA_Q, SLAB_DSA_K, SLAB_DSA_V = 18, 20, 22
SLAB_IQ, SLAB_IK, SLAB_IW = 24, 28, 29
N_SLABS = 30
IN_COLS = 3656
SLABS_PER_DOT = 4

P_MOBA_Q, P_MOBA_K, P_DIFF_Q, P_DIFF_K, P_DSA_Q, P_DSA_K, P_IQ, P_IK = 0, 2, 4, 8, 12, 14, 16, 20
N_P_SLABS = 21
VT_MOBA, VT_DIFF, VT_DSA = 0, 4, 8
N_VT_HEADS = 12
VT_TILE = 256

ATTN_SCALE = HEAD_DIM ** -0.5 * math.log2(math.e)
IDX_SCALE = 64 ** -0.5

VMEM_LIMIT = 56 * 1024 * 1024
ACT_DTYPE = jnp.bfloat16

NEG_INF = float("-inf")
SEARCH_STEPS_PER_CHECK = 4
LOWEST_KEY = -2139095040

_NT = (((1,), (1,)), ((), ()))


def _f32(x):
    return x.astype(jnp.float32)


def _bf16(x):
    return x.astype(ACT_DTYPE)


def _slab_kind(s):
    if s < SLAB_MOBA_K:
        return True, 0, True, ATTN_SCALE, ("p", P_MOBA_Q + s - SLAB_MOBA_Q)
    if s < SLAB_MOBA_V:
        return True, 1, True, 1.0, ("p", P_MOBA_K + s - SLAB_MOBA_K)
    if s < SLAB_DIFF_Q:
        return False, 0, False, 1.0, ("vt", VT_MOBA + 2 * (s - SLAB_MOBA_V), 2)
    if s < SLAB_DIFF_K:
        return True, 2, True, ATTN_SCALE, ("p", P_DIFF_Q + s - SLAB_DIFF_Q)
    if s < SLAB_DIFF_V:
        return True, 3, True, 1.0, ("p", P_DIFF_K + s - SLAB_DIFF_K)
    if s < SLAB_DSA_Q:
        return False, 0, False, 1.0, ("vt", VT_DIFF + s - SLAB_DIFF_V, 1)
    if s < SLAB_DSA_K:
        return True, 4, True, ATTN_SCALE, ("p", P_DSA_Q + s - SLAB_DSA_Q)
    if s < SLAB_DSA_V:
        return True, 5, True, 1.0, ("p", P_DSA_K + s - SLAB_DSA_K)
    if s < SLAB_IQ:
        return False, 0, False, 1.0, ("vt", VT_DSA + 2 * (s - SLAB_DSA_V), 2)
    if s < SLAB_IK:
        return False, 0, True, IDX_SCALE, ("p", P_IQ + s - SLAB_IQ)
    if s < SLAB_IW:
        return False, 0, True, 1.0, ("p", P_IK)
    return False, 0, False, IDX_HEADS ** -0.5, ("iw",)


def _inproj_kernel(x_ref, g_ref, w_ref, cos_ref, sin_ref, hg_ref, p_ref, vt_ref, iw_ref):
    x = x_ref[...]
    ms = jnp.mean(x * x, axis=-1, keepdims=True)
    h = _bf16(x * lax.rsqrt(ms + EPS) * g_ref[...])

    lane = lax.broadcasted_iota(jnp.int32, (1, LANES), 1)
    first_half = (lane & (HEAD_DIM - 1)) < HALF
    r = lax.broadcasted_iota(jnp.int32, (LANES, LANES), 0) // HEAD_DIM
    c = lax.broadcasted_iota(jnp.int32, (LANES, LANES), 1) // HEAD_DIM
    head_ones = _bf16(jnp.where(r == c, 1.0, 0.0))
    cos = cos_ref[...]
    sin = sin_ref[...]
    hg = hg_ref[...]

    tm = x.shape[0]
    ones_rows = jnp.ones((HEAD_DIM, VT_TILE), vt_ref.dtype)
    for g0 in range(0, N_SLABS, SLABS_PER_DOT):
        n = min(SLABS_PER_DOT, N_SLABS - g0)
        yg = jnp.dot(h, w_ref[:, g0 * LANES:(g0 + n) * LANES], preferred_element_type=jnp.float32)
        for k in range(n):
            s = g0 + k
            y = yg[:, k * LANES:(k + 1) * LANES]
            norm, grow, rope, scale, dest = _slab_kind(s)
            if norm:
                yy = y * y
                hi = _bf16(yy)
                lo = _bf16(yy - _f32(hi))
                ss = (jnp.dot(hi, head_ones, preferred_element_type=jnp.float32)
                      + jnp.dot(lo, head_ones, preferred_element_type=jnp.float32))
                y = y * lax.rsqrt(ss * (1.0 / HEAD_DIM) + EPS) * hg[grow:grow + 1, :]
            if rope:
                partner = jnp.where(first_half, pltpu.roll(y, LANES - HALF, 1), pltpu.roll(y, HALF, 1))
                y = y * cos + partner * sin
            if scale != 1.0:
                y = y * scale
            if dest[0] == "p":
                p_ref[:, dest[1] * LANES:(dest[1] + 1) * LANES] = _bf16(y)
            elif dest[0] == "iw":
                iw_ref[...] = y
            else:
                _, head, n_heads = dest
                for r in range(tm // VT_TILE):
                    y_t = _bf16(y[r * VT_TILE:(r + 1) * VT_TILE, :].T)
                    if n_heads == 1:
                        vt_ref[head, r] = y_t
                    else:
                        for hh in range(2):
                            vt_ref[head + hh, r, 0:HEAD_DIM, :] = y_t[hh * HEAD_DIM:(hh + 1) * HEAD_DIM, :]
                            vt_ref[head + hh, r, HEAD_DIM:, :] = ones_rows


def _inproj(x2d, g, w_packed, cos_t, sin_t, head_gains, seq, tm):
    m = x2d.shape[0]
    nc = w_packed.shape[1]
    n_seq_tiles = seq // tm
    return pl.pallas_call(
        _inproj_kernel,
        grid=(m // tm,),
        in_specs=[
            pl.BlockSpec((tm, D_MODEL), lambda i: (i, 0)),
            pl.BlockSpec((1, D_MODEL), lambda i: (0, 0)),
            pl.BlockSpec((D_MODEL, nc), lambda i: (0, 0)),
            pl.BlockSpec((tm, LANES), lambda i: (i % n_seq_tiles, 0)),
            pl.BlockSpec((tm, LANES), lambda i: (i % n_seq_tiles, 0)),
            pl.BlockSpec((8, LANES), lambda i: (0, 0)),
        ],
        out_specs=[
            pl.BlockSpec((tm, N_P_SLABS * LANES), lambda i: (i, 0)),
            pl.BlockSpec((None, N_VT_HEADS, tm // VT_TILE, LANES, VT_TILE),
                         lambda i: (i // n_seq_tiles, 0, i % n_seq_tiles, 0, 0)),
            pl.BlockSpec((tm, LANES), lambda i: (i, 0)),
        ],
        out_shape=[
            jax.ShapeDtypeStruct((m, N_P_SLABS * LANES), ACT_DTYPE),
            jax.ShapeDtypeStruct((m // seq, N_VT_HEADS, seq // VT_TILE, LANES, VT_TILE), ACT_DTYPE),
            jax.ShapeDtypeStruct((m, LANES), jnp.float32),
        ],
        compiler_params=pltpu.CompilerParams(
            dimension_semantics=("parallel",), vmem_limit_bytes=VMEM_LIMIT),
        name="inproj",
    )(x2d, g, w_packed, cos_t, sin_t, head_gains)


def _attend_pipelined(n_tiles, n_maps, t, score_tile, query_keep, value_tile, s_ref, p_ref, acc_ref, sum_p):
    maps = range(n_maps)

    def put_scores(j, slot):
        smax = []
        for c in maps:
            s = score_tile(j, c)
            s_ref[slot, c] = s
            smax.append(jnp.where(query_keep(j, c), jnp.max(s, axis=0, keepdims=True), NEG_INF))
        return tuple(smax)

    def softmax(j, slot, smax, m_old, l_old):
        alpha, m_new, l_new = [], [], []
        for c in maps:
            m = jnp.maximum(m_old[c], smax[c])
            m_safe = jnp.where(m == NEG_INF, 0.0, m)
            a = jnp.exp2(m_old[c] - m_safe)
            p = jnp.exp2(s_ref[slot, c] - jnp.where(query_keep(j, c), m_safe, jnp.inf))
            p_ref[slot, c] = _bf16(p)
            alpha.append(a)
            m_new.append(m)
            l_new.append(a * l_old[c] + jnp.sum(p, axis=0, keepdims=True) if sum_p else l_old[c])
        return tuple(alpha), tuple(m_new), tuple(l_new)

    def add_values(j, slot, alpha):
        for c in maps:
            acc_ref[c] = alpha[c] * acc_ref[c] + jnp.dot(value_tile(j, c), p_ref[slot, c],
                                                         preferred_element_type=jnp.float32)

    def trip(i, carry):
        smax0, alpha_prev, m, l = carry
        j = 2 * i
        smax1 = put_scores(j + 1, 1)
        alpha0, m, l = softmax(j, 0, smax0, m, l)
        add_values(j - 1, 1, alpha_prev)
        smax0 = put_scores(j + 2, 0)
        alpha1, m, l = softmax(j + 1, 1, smax1, m, l)
        add_values(j, 0, alpha0)
        return smax0, alpha1, m, l

    m0 = (jnp.full((1, t), NEG_INF, jnp.float32),) * n_maps
    l0 = (jnp.zeros((1, t), jnp.float32),) * n_maps
    one = (jnp.ones((1, t), jnp.float32),) * n_maps
    for c in maps:
        p_ref[1, c] = jnp.zeros((t, t), p_ref.dtype)
    n_trips = (n_tiles + 1) // 2
    _, alpha_last, _, l_fin = lax.fori_loop(0, n_trips, trip, (put_scores(0, 0), one, m0, l0))
    add_values(2 * n_trips - 1, 1, alpha_last)
    return l_fin


def _causal_bias_tiles(t):
    key = lax.broadcasted_iota(jnp.int32, (t, t), 0)
    qry = lax.broadcasted_iota(jnp.int32, (t, t), 1)
    tri = jnp.where(key <= qry, 0.0, NEG_INF).astype(jnp.float32)
    return jnp.stack([jnp.zeros((t, t), jnp.float32), tri])


def _lane_masks():
    lane = lax.broadcasted_iota(jnp.int32, (1, LANES), 1)
    return lane < HEAD_DIM, lane >= HEAD_DIM


def _split_heads_in_slab(q):
    lo_mask, hi_mask = _lane_masks()
    zero = jnp.zeros_like(q)
    return jnp.where(lo_mask, q, zero), jnp.where(hi_mask, q, zero)


def _causal_tile_mask_t(t):
    key = lax.broadcasted_iota(jnp.int32, (t, t), 0)
    qry = lax.broadcasted_iota(jnp.int32, (t, t), 1)
    return key <= qry


def _moba_kernel(q_ref, k_ref, vt_ref, bias_ref, o_ref, kmean_ref, sel_ref, acc_ref, s_ref, p_ref, *, n_blocks):
    t = MOBA_BLOCK
    qi = pl.program_id(2)

    @pl.when(qi == 0)
    def _():
        kmean_ref[...] = jnp.zeros_like(kmean_ref)
        for j in range(n_blocks):
            kb = _f32(k_ref[j * t:(j + 1) * t, :])
            kmean_ref[j:j + 1, :] = jnp.sum(kb, axis=0, keepdims=True) * (1.0 / t)

    km = kmean_ref[...]
    km_hi = _bf16(km)
    km_lo = _bf16(km - _f32(km_hi))
    blk = lax.broadcasted_iota(jnp.int32, (LANES, t), 0)
    blk_f = _f32(blk)
    n_sel = min(MOBA_TOPK, n_blocks - 1)
    qhs = _split_heads_in_slab(q_ref[...])

    for hd, qh in enumerate(qhs):
        gate = (lax.dot_general(km_hi, qh, _NT, preferred_element_type=jnp.float32)
                + lax.dot_general(km_lo, qh, _NT, preferred_element_type=jnp.float32))
        gate = jnp.where(blk < qi, gate, NEG_INF)
        sel = jnp.zeros((LANES, t), jnp.float32)
        for rank in range(n_sel):
            gmax = jnp.max(gate, axis=0, keepdims=True)
            first = jnp.min(jnp.where(gate == gmax, blk_f, float(LANES)), axis=0, keepdims=True)
            pick = blk_f == first
            sel = jnp.where(pick, jnp.maximum(sel, jnp.where(rank < qi, 1.0, 0.0)), sel)
            gate = jnp.where(pick, NEG_INF, gate)
        sel_ref[hd] = jnp.where(blk == qi, 1.0, sel)
        acc_ref[hd] = jnp.zeros((LANES, t), jnp.float32)

    def score_tile(j, c):
        kj = k_ref[pl.ds(pl.multiple_of(jnp.minimum(j, qi) * t, t), t), :]
        s = lax.dot_general(kj, qhs[c], _NT, preferred_element_type=jnp.float32)
        return s + bias_ref[jnp.where(j >= qi, 1, 0)]

    def query_keep(j, c):
        return sel_ref[c, pl.ds(jnp.minimum(j, LANES - 1), 1), :] > 0.0

    def value_tile(j, c):
        return vt_ref[c, jnp.clip(j, 0, qi)]

    _attend_pipelined(qi + 1, 2, t, score_tile, query_keep, value_tile, s_ref, p_ref, acc_ref, False)
    outs = [acc_ref[c, 0:HEAD_DIM, :] / acc_ref[c, HEAD_DIM:HEAD_DIM + 1, :] for c in range(2)]
    o_ref[...] = _bf16(jnp.concatenate(outs, axis=0).T)


def _moba(p3, vt, seq):
    b = p3.shape[0]
    t = MOBA_BLOCK
    n_blocks = seq // t
    return pl.pallas_call(
        functools.partial(_moba_kernel, n_blocks=n_blocks),
        grid=(b, 2, n_blocks),
        in_specs=[
            pl.BlockSpec((None, t, LANES), lambda bi, s, qi: (bi, qi, P_MOBA_Q + s)),
            pl.BlockSpec((None, seq, LANES), lambda bi, s, qi: (bi, 0, P_MOBA_K + s)),
            pl.BlockSpec((None, 2, n_blocks, LANES, t), lambda bi, s, qi: (bi, VT_MOBA // 2 + s, 0, 0, 0)),
            pl.BlockSpec((2, t, t), lambda bi, s, qi: (0, 0, 0)),
        ],
        out_specs=pl.BlockSpec((None, t, LANES), lambda bi, s, qi: (bi, qi, s)),
        out_shape=jax.ShapeDtypeStruct((b, seq, 2 * LANES), ACT_DTYPE),
        scratch_shapes=[
            pltpu.VMEM((LANES, LANES), jnp.float32),
            pltpu.VMEM((2, LANES, t), jnp.float32),
            pltpu.VMEM((2, LANES, t), jnp.float32),
            pltpu.VMEM((2, 2, t, t), jnp.float32),
            pltpu.VMEM((2, 2, t, t), ACT_DTYPE),
        ],
        compiler_params=pltpu.CompilerParams(
            dimension_semantics=("parallel", "parallel", "arbitrary"), vmem_limit_bytes=VMEM_LIMIT),
        name="moba",
    )(p3, p3, vt, _causal_bias_tiles(t))


def _diff_kernel(lam_ref, q_ref, k_ref, vt_ref, sg_ref, bias_ref, o_ref, acc_ref, s_ref, p_ref, *, t, lam_init):
    qi = pl.program_id(2)
    qs = _split_heads_in_slab(q_ref[...])

    def score_tile(j, c):
        kj = k_ref[pl.ds(pl.multiple_of(jnp.minimum(j, qi) * t, t), t), :]
        s = lax.dot_general(kj, qs[c], _NT, preferred_element_type=jnp.float32)
        return s + bias_ref[jnp.where(j >= qi, 1, 0)]

    def query_keep(j, c):
        return jnp.broadcast_to(j <= qi, (1, t))

    def value_tile(j, c):
        return vt_ref[jnp.clip(j, 0, qi)]

    for c in range(2):
        acc_ref[c] = jnp.zeros((LANES, t), jnp.float32)
    l1, l2 = _attend_pipelined(qi + 1, 2, t, score_tile, query_keep, value_tile, s_ref, p_ref, acc_ref, True)

    lp = lam_ref[...]
    lam = (jnp.exp(jnp.sum(lp[0:1, :] * lp[1:2, :], axis=-1, keepdims=True))
           - jnp.exp(jnp.sum(lp[2:3, :] * lp[3:4, :], axis=-1, keepdims=True)) + lam_init)
    a = (acc_ref[0] / l1 - lam * (acc_ref[1] / l2)).T
    ms = jnp.mean(a * a, axis=-1, keepdims=True)
    o_ref[...] = _bf16(a * lax.rsqrt(ms + EPS) * sg_ref[...] * (1.0 - lam_init))


def _diff(p3, vt, lam_params, subln_g, seq, lam_init, t):
    b = p3.shape[0]
    n_heads = 4
    return pl.pallas_call(
        functools.partial(_diff_kernel, t=t, lam_init=lam_init),
        grid=(b, n_heads, seq // t),
        in_specs=[
            pl.BlockSpec((4, HEAD_DIM), lambda bi, h, qi: (0, 0)),
            pl.BlockSpec((None, t, LANES), lambda bi, h, qi: (bi, qi, P_DIFF_Q + h)),
            pl.BlockSpec((None, seq, LANES), lambda bi, h, qi: (bi, 0, P_DIFF_K + h)),
            pl.BlockSpec((None, None, seq // t, LANES, t), lambda bi, h, qi: (bi, VT_DIFF + h, 0, 0, 0)),
            pl.BlockSpec((1, LANES), lambda bi, h, qi: (0, 0)),
            pl.BlockSpec((2, t, t), lambda bi, h, qi: (0, 0, 0)),
        ],
        out_specs=pl.BlockSpec((None, t, LANES), lambda bi, h, qi: (bi, qi, h)),
        out_shape=jax.ShapeDtypeStruct((b, seq, n_heads * LANES), ACT_DTYPE),
        scratch_shapes=[pltpu.VMEM((2, LANES, t), jnp.float32),
                        pltpu.VMEM((2, 2, t, t), jnp.float32),
                        pltpu.VMEM((2, 2, t, t), ACT_DTYPE)],
        compiler_params=pltpu.CompilerParams(
            dimension_semantics=("parallel", "parallel", "parallel"), vmem_limit_bytes=VMEM_LIMIT),
        name="diff",
    )(lam_params, p3, p3, vt, subln_g, _causal_bias_tiles(t))


def _key_to_f32(key):
    bits = key ^ ((key >> 31) & jnp.int32(0x7FFFFFFF))
    return lax.bitcast_convert_type(bits, jnp.float32)


def _dsa_kernel(iq_ref, ik_ref, iw_ref, q_ref, k_ref, vt_ref, o_ref,
                score_ref, acc_ref, j_ref, s_ref, p_ref, *, t, n_top, idx_bits):
    qi = pl.program_id(1)
    n_chunks = qi + 1
    causal = _causal_tile_mask_t(t)
    key_in_tile = lax.broadcasted_iota(jnp.int32, (t, t), 0)

    iw_t = iw_ref[...].T
    iq_heads = []
    for slab in range(IDX_HEADS // 2):
        iq_heads.extend(_split_heads_in_slab(iq_ref[:, slab * LANES:(slab + 1) * LANES]))

    def index_chunk(c, diag):
        kk = ik_ref[pl.ds(pl.multiple_of(c * t, t), t), :]
        sc = jnp.zeros((t, t), jnp.float32)
        for h in range(IDX_HEADS):
            lg = lax.dot_general(kk, iq_heads[h], _NT, preferred_element_type=jnp.float32)
            sc = sc + jnp.maximum(lg, 0.0) * iw_t[h:h + 1, :]
        if diag:
            sc = jnp.where(causal, sc, NEG_INF)
        score_ref[c] = sc

    def index_body(c, carry):
        index_chunk(c, False)
        return carry

    lax.fori_loop(0, qi, index_body, 0)
    index_chunk(qi, True)

    @pl.when(n_chunks % 2 == 1)
    def _():
        score_ref[n_chunks] = jnp.full((t, t), NEG_INF, jnp.float32)

    n_pairs = (n_chunks + 1) // 2

    def count_keys(hits):
        def body(i, acc):
            for c in (2 * i, 2 * i + 1):
                acc = acc + jnp.sum(hits(score_ref[c], c).reshape(t // 8, 8, t), axis=0)
            return acc
        acc = lax.fori_loop(0, n_pairs, body, jnp.zeros((8, t), jnp.float32))
        return jnp.sum(acc, axis=0, keepdims=True)

    def count_ge(cand_key):
        cand = _key_to_f32(cand_key)
        return count_keys(lambda s, c: jnp.where(s >= cand, 1.0, 0.0))

    k_top = jnp.float32(n_top)
    lowest = _key_to_f32(jnp.int32(LOWEST_KEY))

    def survey(c, carry):
        mx, n_fin, n_ge0, n_gt0 = carry
        s = score_ref[c]
        fold = lambda x: x.reshape(t // 8, 8, t)
        return (jnp.maximum(mx, jnp.max(fold(s), axis=0)),
                n_fin + jnp.sum(fold(jnp.where(s >= lowest, 1.0, 0.0)), axis=0),
                n_ge0 + jnp.sum(fold(jnp.where(s >= 0.0, 1.0, 0.0)), axis=0),
                n_gt0 + jnp.sum(fold(jnp.where(s > 0.0, 1.0, 0.0)), axis=0))

    zeros8 = jnp.zeros((8, t), jnp.float32)
    mx, n_fin, n_ge0, n_gt0 = lax.fori_loop(
        0, n_chunks, survey, (jnp.full((8, t), NEG_INF, jnp.float32), zeros8, zeros8, zeros8))
    mx = jnp.max(mx, axis=0, keepdims=True)
    n_fin, n_ge0, n_gt0 = (jnp.sum(v, axis=0, keepdims=True) for v in (n_fin, n_ge0, n_gt0))

    mx_bits = lax.bitcast_convert_type(mx, jnp.int32)
    mx_key = mx_bits ^ ((mx_bits >> 31) & jnp.int32(0x7FFFFFFF))
    few = n_fin <= k_top
    at_zero = n_ge0 >= k_top
    zero_key = jnp.zeros((1, t), jnp.int32)
    lo = jnp.where(few, LOWEST_KEY, jnp.where(at_zero, zero_key, LOWEST_KEY))
    c_lo = jnp.where(few, n_fin, jnp.where(at_zero, n_ge0, n_fin))
    hi = jnp.where(at_zero, jnp.where(n_gt0 < k_top, zero_key + 1, mx_key + 1), zero_key)
    hi = jnp.where(few, lo + 1, hi)

    def still_active(lo, hi, c_lo):
        return jnp.where(c_lo == k_top, 0.0, jnp.where(hi - lo <= 1, 0.0, 1.0))

    def search_cond(state):
        it, n_active = state[0], state[1]
        return jnp.logical_and(n_active > 0.0, it < 34)

    def search_step(state):
        it, _, lo, hi, c_lo = state
        for _ in range(SEARCH_STEPS_PER_CHECK):
            active = still_active(lo, hi, c_lo) > 0.0
            cand = lo + ((hi - lo) >> 1)
            cnt = count_ge(cand)
            ge = cnt >= k_top
            lo, hi, c_lo = (jnp.where(active, jnp.where(ge, cand, lo), lo),
                            jnp.where(active, jnp.where(ge, hi, cand), hi),
                            jnp.where(active, jnp.where(ge, cnt, c_lo), c_lo))
        return it + SEARCH_STEPS_PER_CHECK, jnp.sum(still_active(lo, hi, c_lo)), lo, hi, c_lo

    _, _, lo, hi, c_lo = lax.while_loop(
        search_cond, search_step, (jnp.int32(0), jnp.sum(still_active(lo, hi, c_lo)), lo, hi, c_lo))
    thr = _key_to_f32(lo)

    j_ref[...] = jnp.full(j_ref.shape, 2 ** 30, jnp.int32)

    @pl.when(jnp.max(jnp.where(c_lo > k_top, 1.0, 0.0)) > 0.0)
    def _():
        need = k_top - count_keys(lambda s, c: jnp.where(s > thr, 1.0, 0.0))

        def jbisect(it, lo):
            cand = lo + (jnp.int32(1) << (idx_bits - 1 - it))
            cnt = count_keys(lambda s, c: jnp.where(
                s == thr, jnp.where((key_in_tile + c * t) <= cand, 1.0, 0.0), 0.0))
            return jnp.where(cnt < need, cand, lo)
        lo = lax.fori_loop(0, idx_bits, jbisect, jnp.full((1, t), -1, jnp.int32))
        j_ref[...] = jnp.broadcast_to(lo + 1, j_ref.shape)

    j_last = j_ref[0:1, :]

    n_heads = 4
    qhs = []
    for slab in range(n_heads // 2):
        qhs.extend(_split_heads_in_slab(q_ref[:, slab * LANES:(slab + 1) * LANES]))
    for hd in range(n_heads):
        acc_ref[hd] = jnp.zeros((LANES, t), jnp.float32)

    shared = {}

    def score_tile(j, hd):
        jc = jnp.minimum(j, qi)
        if hd == 0:
            s_idx = score_ref[jc]
            keep_tie = jnp.where((key_in_tile + jc * t) <= j_last, 0.0, NEG_INF)
            shared["bias"] = jnp.where(s_idx > thr, 0.0, jnp.where(s_idx == thr, keep_tie, NEG_INF))
        slab = hd // 2
        kj = k_ref[pl.ds(pl.multiple_of(jc * t, t), t), slab * LANES:(slab + 1) * LANES]
        return lax.dot_general(kj, qhs[hd], _NT, preferred_element_type=jnp.float32) + shared["bias"]

    def query_keep(j, hd):
        return jnp.broadcast_to(j <= qi, (1, t))

    def value_tile(j, hd):
        return vt_ref[hd, jnp.clip(j, 0, qi)]

    _attend_pipelined(n_chunks, n_heads, t, score_tile, query_keep, value_tile, s_ref, p_ref, acc_ref, False)
    outs = [acc_ref[hd, 0:HEAD_DIM, :] / acc_ref[hd, HEAD_DIM:HEAD_DIM + 1, :] for hd in range(n_heads)]
    for slab in range(n_heads // 2):
        o_ref[:, slab * LANES:(slab + 1) * LANES] = _bf16(jnp.concatenate(outs[2 * slab:2 * slab + 2], axis=0).T)


def _dsa(p3, vt, iw3, seq, t):
    b = p3.shape[0]
    n_top = min(DSA_TOPK_MAX, seq // 4)
    idx_bits = max(1, (seq - 1).bit_length())
    n_tiles = seq // t
    return pl.pallas_call(
        functools.partial(_dsa_kernel, t=t, n_top=n_top, idx_bits=idx_bits),
        grid=(b, n_tiles),
        in_specs=[
            pl.BlockSpec((None, t, 4 * LANES), lambda bi, qi: (bi, qi, P_IQ // 4)),
            pl.BlockSpec((None, seq, LANES), lambda bi, qi: (bi, 0, P_IK)),
            pl.BlockSpec((None, t, LANES), lambda bi, qi: (bi, qi, 0)),
            pl.BlockSpec((None, t, 2 * LANES), lambda bi, qi: (bi, qi, P_DSA_Q // 2)),
            pl.BlockSpec((None, seq, 2 * LANES), lambda bi, qi: (bi, 0, P_DSA_K // 2)),
            pl.BlockSpec((None, 4, n_tiles, LANES, t), lambda bi, qi: (bi, VT_DSA // 4, 0, 0, 0)),
        ],
        out_specs=pl.BlockSpec((None, t, 2 * LANES), lambda bi, qi: (bi, qi, 0)),
        out_shape=jax.ShapeDtypeStruct((b, seq, 2 * LANES), ACT_DTYPE),
        scratch_shapes=[
            pltpu.VMEM((n_tiles, t, t), jnp.float32),
            pltpu.VMEM((4, LANES, t), jnp.float32),
            pltpu.VMEM((8, t), jnp.int32),
            pltpu.VMEM((2, 4, t, t), jnp.float32),
            pltpu.VMEM((2, 4, t, t), ACT_DTYPE),
        ],
        compiler_params=pltpu.CompilerParams(
            dimension_semantics=("parallel", "parallel"), vmem_limit_bytes=VMEM_LIMIT),
        name="dsa",
    )(p3, p3, iw3, p3, p3, vt)


def _ffn_kernel(x_ref, moba_ref, diff_ref, dsa_ref, wo_ref, g_ref, wg_ref, wu_ref, wd_ref, o_ref, *, ff_chunk):
    w_moba, w_diff = 2 * LANES, 4 * LANES
    y = (x_ref[...]
         + jnp.dot(moba_ref[...], wo_ref[0:w_moba, :], preferred_element_type=jnp.float32)
         + jnp.dot(diff_ref[...], wo_ref[w_moba:w_moba + w_diff, :], preferred_element_type=jnp.float32)
         + jnp.dot(dsa_ref[...], wo_ref[w_moba + w_diff:, :], preferred_element_type=jnp.float32))
    ms = jnp.mean(y * y, axis=-1, keepdims=True)
    h = _bf16(y * lax.rsqrt(ms + EPS) * g_ref[...])
    o_ref[...] = y
    for c in range(D_FF // ff_chunk):
        sl = slice(c * ff_chunk, (c + 1) * ff_chunk)
        gate = jnp.dot(h, wg_ref[:, sl], preferred_element_type=jnp.float32)
        up = jnp.dot(h, wu_ref[:, sl], preferred_element_type=jnp.float32)
        act = _bf16(gate * (1.0 / (1.0 + jnp.exp(-gate))) * up)
        o_ref[...] += jnp.dot(act, wd_ref[sl, :], preferred_element_type=jnp.float32)


def _ffn(x2d, moba, diff, dsa, wo, g, wg, wu, wd, tm, ff_chunk):
    m = x2d.shape[0]
    const = lambda i: (0, 0)
    resident = dict(pipeline_mode=pl.Buffered(1))
    return pl.pallas_call(
        functools.partial(_ffn_kernel, ff_chunk=ff_chunk),
        grid=(m // tm,),
        in_specs=[
            pl.BlockSpec((tm, D_MODEL), lambda i: (i, 0)),
            pl.BlockSpec((tm, 2 * LANES), lambda i: (i, 0)),
            pl.BlockSpec((tm, 4 * LANES), lambda i: (i, 0)),
            pl.BlockSpec((tm, 2 * LANES), lambda i: (i, 0)),
            pl.BlockSpec((D_MODEL, D_MODEL), const, **resident),
            pl.BlockSpec((1, D_MODEL), const),
            pl.BlockSpec((D_MODEL, D_FF), const, **resident),
            pl.BlockSpec((D_MODEL, D_FF), const, **resident),
            pl.BlockSpec((D_FF, D_MODEL), const, **resident),
        ],
        out_specs=pl.BlockSpec((tm, D_MODEL), lambda i: (i, 0)),
        out_shape=jax.ShapeDtypeStruct((m, D_MODEL), jnp.float32),
        compiler_params=pltpu.CompilerParams(
            dimension_semantics=("parallel",), vmem_limit_bytes=VMEM_LIMIT),
        name="outproj_ffn",
    )(x2d, moba, diff, dsa, wo, g, wg, wu, wd)


def _rope_tables(seq):
    inv = 1.0 / (ROPE_THETA ** (jnp.arange(0, HEAD_DIM, 2, dtype=jnp.float32) / HEAD_DIM))
    ang = jnp.arange(seq, dtype=jnp.float32)[:, None] * inv[None, :]
    cos, sin = jnp.cos(ang), jnp.sin(ang)
    return jnp.tile(cos, (1, 4)), jnp.concatenate([-sin, sin, -sin, sin], axis=1)


def _pack_w_in(w):
    ik = w[:, SLAB_IK * LANES:SLAB_IK * LANES + HEAD_DIM]
    iw = w[:, SLAB_IK * LANES + HEAD_DIM:IN_COLS]
    pad = jnp.zeros((w.shape[0], LANES - IDX_HEADS), w.dtype)
    return _bf16(jnp.concatenate([w[:, :SLAB_IK * LANES], ik, ik, iw, pad], axis=1))


def kernel(x, attn_norm_g, w_in, q_norm_g, k_norm_g, diff_lambda, diff_subln_g,
           w_out, ffn_norm_g, w_gate, w_up, w_down):
    b, seq, d = x.shape
    depth = w_in.shape[0]
    assert d == D_MODEL and seq % MOBA_BLOCK == 0
    tq = MOBA_BLOCK
    tm_in = 512
    tm_ffn = 512
    cos_t, sin_t = _rope_tables(seq)
    x2d = x.reshape(b * seq, d)

    for l in range(depth):
        gains = jnp.stack([q_norm_g[l, 0], k_norm_g[l, 0], q_norm_g[l, 1], k_norm_g[l, 1],
                           q_norm_g[l, 2], k_norm_g[l, 2]])
        head_gains = jnp.concatenate(
            [jnp.tile(gains, (1, 2)), jnp.ones((2, LANES), jnp.float32)], axis=0)
        p, vt, iw = _inproj(x2d, attn_norm_g[l][None, :], _pack_w_in(w_in[l]), cos_t, sin_t,
                            head_gains, seq, tm_in)
        p3 = p.reshape(b, seq, N_P_SLABS * LANES)
        iw3 = iw.reshape(b, seq, LANES)

        moba = _moba(p3, vt, seq)
        lam_init = 0.8 - 0.6 * math.exp(-0.3 * l)
        diff = _diff(p3, vt, diff_lambda[l], diff_subln_g[l][None, :], seq, lam_init, tq)
        dsa = _dsa(p3, vt, iw3, seq, tq)

        x2d = _ffn(x2d, moba.reshape(b * seq, -1), diff.reshape(b * seq, -1), dsa.reshape(b * seq, -1),
                   _bf16(w_out[l]), ffn_norm_g[l][None, :], _bf16(w_gate[l]), _bf16(w_up[l]),
                   _bf16(w_down[l]), tm_ffn, 704)
    return x2d.reshape(b, seq, d)
```

```python
import functools
import math

import jax
import jax.numpy as jnp
from jax import lax
from jax.experimental import pallas as pl
from jax.experimental.pallas import tpu as pltpu

D_MODEL = 1024
HEAD_DIM = 64
MOBA_BLOCK = 256
MOBA_TOPK = 3
DSA_TOPK_MAX = 256
IDX_HEADS = 8
ROPE_THETA = 10000.0
EPS = 1e-6
D_FF = 2816

LANES = 128
HALF = HEAD_DIM // 2

SLAB_MOBA_Q, SLAB_MOBA_K, SLAB_MOBA_V = 0, 2, 4
SLAB_DIFF_Q, SLAB_DIFF_K, SLAB_DIFF_V = 6, 10, 14
SLAB_DSA_Q, SLAB_DSA_K, SLAB_DSA_V = 18, 20, 22
SLAB_IQ, SLAB_IK, SLAB_IW = 24, 28, 29
N_SLABS = 30
IN_COLS = 3656
SLABS_PER_DOT = 4

P_MOBA_Q, P_MOBA_K, P_DIFF_Q, P_DIFF_K, P_DSA_Q, P_DSA_K, P_IQ, P_IK = 0, 2, 4, 8, 12, 14, 16, 20
N_P_SLABS = 21
VT_MOBA, VT_DSA = 0, 4
N_VT_HEADS = 8
N_DIFF_HEADS = 4
VT_DIFF_ROWS = LANES + 16
VT_TILE = 256
ROW_TILE = 512

ATTN_SCALE = HEAD_DIM ** -0.5 * math.log2(math.e)
IDX_SCALE = 64 ** -0.5

VMEM_LIMIT = 56 * 1024 * 1024
ACT_DTYPE = jnp.bfloat16

NEG_INF = float("-inf")
SEARCH_STEPS_PER_CHECK = 4
LOWEST_KEY = -2139095040

_NT = (((1,), (1,)), ((), ()))


def _f32(x):
    return x.astype(jnp.float32)


def _bf16(x):
    return x.astype(ACT_DTYPE)


def _slab_kind(s):
    if s < SLAB_MOBA_K:
        return True, 0, True, ATTN_SCALE, ("p", P_MOBA_Q + s - SLAB_MOBA_Q)
    if s < SLAB_MOBA_V:
        return True, 1, True, 1.0, ("p", P_MOBA_K + s - SLAB_MOBA_K)
    if s < SLAB_DIFF_Q:
        return False, 0, False, 1.0, ("vt", VT_MOBA + 2 * (s - SLAB_MOBA_V))
    if s < SLAB_DIFF_K:
        return True, 2, True, ATTN_SCALE, ("p", P_DIFF_Q + s - SLAB_DIFF_Q)
    if s < SLAB_DIFF_V:
        return True, 3, True, 1.0, ("p", P_DIFF_K + s - SLAB_DIFF_K)
    if s < SLAB_DSA_Q:
        return False, 0, False, 1.0, ("vtd", s - SLAB_DIFF_V)
    if s < SLAB_DSA_K:
        return True, 4, True, ATTN_SCALE, ("p", P_DSA_Q + s - SLAB_DSA_Q)
    if s < SLAB_DSA_V:
        return True, 5, True, 1.0, ("p", P_DSA_K + s - SLAB_DSA_K)
    if s < SLAB_IQ:
        return False, 0, False, 1.0, ("vt", VT_DSA + 2 * (s - SLAB_DSA_V))
    if s < SLAB_IK:
        return False, 0, True, IDX_SCALE, ("p", P_IQ + s - SLAB_IQ)
    if s < SLAB_IW:
        return False, 0, True, 1.0, ("p", P_IK)
    return False, 0, False, IDX_HEADS ** -0.5, ("iw",)


def _inproj_kernel(x_ref, g_ref, w_ref, cos_ref, sin_ref, hg_ref, p_ref, vt_ref, vtd_ref, iw_ref):
    x = x_ref[...]
    ms = jnp.mean(x * x, axis=-1, keepdims=True)
    h = _bf16(x * lax.rsqrt(ms + EPS) * g_ref[...])

    lane = lax.broadcasted_iota(jnp.int32, (1, LANES), 1)
    first_half = (lane & (HEAD_DIM - 1)) < HALF
    r = lax.broadcasted_iota(jnp.int32, (LANES, LANES), 0) // HEAD_DIM
    c = lax.broadcasted_iota(jnp.int32, (LANES, LANES), 1) // HEAD_DIM
    head_ones = _bf16(jnp.where(r == c, 1.0, 0.0))
    cos = cos_ref[...]
    sin = sin_ref[...]
    hg = hg_ref[...]

    tm = x.shape[0]
    ones_rows = jnp.ones((HEAD_DIM, VT_TILE), vt_ref.dtype)
    for g0 in range(0, N_SLABS, SLABS_PER_DOT):
        n = min(SLABS_PER_DOT, N_SLABS - g0)
        yg = jnp.dot(h, w_ref[:, g0 * LANES:(g0 + n) * LANES], preferred_element_type=jnp.float32)
        for k in range(n):
            s = g0 + k
            y = yg[:, k * LANES:(k + 1) * LANES]
            norm, grow, rope, scale, dest = _slab_kind(s)
            if norm:
                yy = y * y
                hi = _bf16(yy)
                lo = _bf16(yy - _f32(hi))
                ss = (jnp.dot(hi, head_ones, preferred_element_type=jnp.float32)
                      + jnp.dot(lo, head_ones, preferred_element_type=jnp.float32))
                y = y * lax.rsqrt(ss * (1.0 / HEAD_DIM) + EPS) * hg[grow:grow + 1, :]
            if rope:
                partner = jnp.where(first_half, pltpu.roll(y, LANES - HALF, 1), pltpu.roll(y, HALF, 1))
                y = y * cos + partner * sin
            if scale != 1.0:
                y = y * scale
            if dest[0] == "p":
                p_ref[:, dest[1] * LANES:(dest[1] + 1) * LANES] = _bf16(y)
            elif dest[0] == "iw":
                iw_ref[...] = y
            else:
                head = dest[1]
                for r in range(tm // VT_TILE):
                    y_t = _bf16(y[r * VT_TILE:(r + 1) * VT_TILE, :].T)
                    if dest[0] == "vtd":
                        vtd_ref[head, r, 0:LANES, :] = y_t
                        vtd_ref[head, r, LANES:, :] = ones_rows[0:VT_DIFF_ROWS - LANES, :]
                    else:
                        for hh in range(2):
                            vt_ref[head + hh, r, 0:HEAD_DIM, :] = y_t[hh * HEAD_DIM:(hh + 1) * HEAD_DIM, :]
                            vt_ref[head + hh, r, HEAD_DIM:, :] = ones_rows


def _inproj(x2d, g, w_packed, cos_t, sin_t, head_gains, seq, tm):
    m = x2d.shape[0]
    nc = w_packed.shape[1]
    n_seq_tiles = seq // tm
    return pl.pallas_call(
        _inproj_kernel,
        grid=(m // tm,),
        in_specs=[
            pl.BlockSpec((tm, D_MODEL), lambda i: (i, 0)),
            pl.BlockSpec((1, D_MODEL), lambda i: (0, 0)),
            pl.BlockSpec((D_MODEL, nc), lambda i: (0, 0)),
            pl.BlockSpec((tm, LANES), lambda i: (i % n_seq_tiles, 0)),
            pl.BlockSpec((tm, LANES), lambda i: (i % n_seq_tiles, 0)),
            pl.BlockSpec((8, LANES), lambda i: (0, 0)),
        ],
        out_specs=[
            pl.BlockSpec((tm, N_P_SLABS * LANES), lambda i: (i, 0)),
            pl.BlockSpec((None, N_VT_HEADS, tm // VT_TILE, LANES, VT_TILE),
                         lambda i: (i // n_seq_tiles, 0, i % n_seq_tiles, 0, 0)),
            pl.BlockSpec((None, N_DIFF_HEADS, tm // VT_TILE, VT_DIFF_ROWS, VT_TILE),
                         lambda i: (i // n_seq_tiles, 0, i % n_seq_tiles, 0, 0)),
            pl.BlockSpec((tm, LANES), lambda i: (i, 0)),
        ],
        out_shape=[
            jax.ShapeDtypeStruct((m, N_P_SLABS * LANES), ACT_DTYPE),
            jax.ShapeDtypeStruct((m // seq, N_VT_HEADS, seq // VT_TILE, LANES, VT_TILE), ACT_DTYPE),
            jax.ShapeDtypeStruct((m // seq, N_DIFF_HEADS, seq // VT_TILE, VT_DIFF_ROWS, VT_TILE), ACT_DTYPE),
            jax.ShapeDtypeStruct((m, LANES), jnp.float32),
        ],
        compiler_params=pltpu.CompilerParams(
            dimension_semantics=("parallel",), vmem_limit_bytes=VMEM_LIMIT),
        name="inproj",
    )(x2d, g, w_packed, cos_t, sin_t, head_gains)


def _attend_pipelined(n_tiles, first_masked, n_maps, score_tile, query_keep, value_tile,
                      s_ref, p_ref, acc_ref):
    maps = range(n_maps)

    def put_scores(j, slot, masked):
        smax = []
        for c in maps:
            s = score_tile(j, c, masked)
            s_ref[slot, c] = s
            smax.append(jnp.where(query_keep(j, c), jnp.max(s, axis=0, keepdims=True), NEG_INF))
        return tuple(smax)

    def softmax(j, slot, smax, m_old):
        alpha, m_new = [], []
        for c in maps:
            m = jnp.maximum(m_old[c], smax[c])
            m_safe = jnp.where(m == NEG_INF, 0.0, m)
            alpha.append(jnp.exp2(m_old[c] - m_safe))
            p = jnp.exp2(s_ref[slot, c] - jnp.where(query_keep(j, c), m_safe, jnp.inf))
            p_ref[slot, c] = _bf16(p)
            m_new.append(m)
        return tuple(alpha), tuple(m_new)

    def add_values(j, slot, alpha):
        for c in maps:
            acc_ref[c] = alpha[c] * acc_ref[c] + jnp.dot(value_tile(j, c), p_ref[slot, c],
                                                         preferred_element_type=jnp.float32)

    def trip(masked, i, carry):
        smax0, alpha_prev, m = carry
        j = 2 * i
        smax1 = put_scores(j + 1, 1, masked)
        alpha0, m = softmax(j, 0, smax0, m)
        add_values(j - 1, 1, alpha_prev)
        smax0 = put_scores(j + 2, 0, masked)
        alpha1, m = softmax(j + 1, 1, smax1, m)
        add_values(j, 0, alpha0)
        return smax0, alpha1, m

    tk, tq = p_ref.shape[2:]
    m0 = (jnp.full((1, tq), NEG_INF, jnp.float32),) * n_maps
    one = (jnp.ones((1, tq), jnp.float32),) * n_maps
    for c in maps:
        p_ref[1, c] = jnp.zeros((tk, tq), p_ref.dtype)
    n_trips = (n_tiles + 1) // 2
    carry = (put_scores(0, 0, True), one, m0)
    n_plain = 0
    if first_masked is not None:
        n_plain = jnp.clip((first_masked - 1) // 2, 0, n_trips)
        carry = lax.fori_loop(0, n_plain, functools.partial(trip, False), carry)
    _, alpha_last, _ = lax.fori_loop(n_plain, n_trips, functools.partial(trip, True), carry)
    add_values(2 * n_trips - 1, 1, alpha_last)


def _causal_bias_tiles(tk, tq):
    key = lax.broadcasted_iota(jnp.int32, (tk, tq), 0)
    qry = lax.broadcasted_iota(jnp.int32, (tk, tq), 1)
    tiles = [jnp.zeros((tk, tq), jnp.float32)]
    for r in range(tq // tk):
        tiles.append(jnp.where(key + r * tk <= qry, 0.0, NEG_INF).astype(jnp.float32))
    return jnp.stack(tiles)


def _lane_masks():
    lane = lax.broadcasted_iota(jnp.int32, (1, LANES), 1)
    return lane < HEAD_DIM, lane >= HEAD_DIM


def _split_heads_in_slab(q):
    lo_mask, hi_mask = _lane_masks()
    zero = jnp.zeros_like(q)
    return jnp.where(lo_mask, q, zero), jnp.where(hi_mask, q, zero)


def _moba_kernel(q_ref, k_ref, vt_ref, bias_ref, o_ref, kmean_ref, sel_ref, acc_ref, s_ref, p_ref, *, n_blocks):
    t = MOBA_BLOCK
    qi = pl.program_id(2)

    @pl.when(qi == 0)
    def _():
        kmean_ref[...] = jnp.zeros_like(kmean_ref)
        for j in range(n_blocks):
            kb = _f32(k_ref[j * t:(j + 1) * t, :])
            kmean_ref[j:j + 1, :] = jnp.sum(kb, axis=0, keepdims=True) * (1.0 / t)

    km = kmean_ref[...]
    km_hi = _bf16(km)
    km_lo = _bf16(km - _f32(km_hi))
    n_rows = kmean_ref.shape[0]
    blk = lax.broadcasted_iota(jnp.int32, (n_rows, t), 0)
    blk_f = _f32(blk)
    n_sel = min(MOBA_TOPK, n_blocks - 1)
    qhs = _split_heads_in_slab(q_ref[...])

    for hd, qh in enumerate(qhs):
        gate = (lax.dot_general(km_hi, qh, _NT, preferred_element_type=jnp.float32)
                + lax.dot_general(km_lo, qh, _NT, preferred_element_type=jnp.float32))
        gate = jnp.where(blk < qi, gate, NEG_INF)
        sel = jnp.zeros((n_rows, t), jnp.float32)
        for rank in range(n_sel):
            gmax = jnp.max(gate, axis=0, keepdims=True)
            first = jnp.min(jnp.where(gate == gmax, blk_f, float(n_rows)), axis=0, keepdims=True)
            pick = blk_f == first
            sel = jnp.where(pick, jnp.maximum(sel, jnp.where(rank < qi, 1.0, 0.0)), sel)
            gate = jnp.where(pick, NEG_INF, gate)
        sel_ref[hd] = jnp.where(blk == qi, 1.0, sel)
        acc_ref[hd] = jnp.zeros((LANES, t), jnp.float32)

    def score_tile(j, c, masked):
        kj = k_ref[pl.ds(pl.multiple_of(jnp.minimum(j, qi) * t, t), t), :]
        s = lax.dot_general(kj, qhs[c], _NT, preferred_element_type=jnp.float32)
        return s + bias_ref[jnp.where(j >= qi, 1, 0)] if masked else s

    def query_keep(j, c):
        return sel_ref[c, pl.ds(jnp.minimum(j, n_rows - 1), 1), :] > 0.0

    def value_tile(j, c):
        return vt_ref[c, jnp.clip(j, 0, qi)]

    _attend_pipelined(qi + 1, qi, 2, score_tile, query_keep, value_tile, s_ref, p_ref, acc_ref)
    outs = [acc_ref[c, 0:HEAD_DIM, :] / acc_ref[c, HEAD_DIM:HEAD_DIM + 1, :] for c in range(2)]
    o_ref[...] = _bf16(jnp.concatenate(outs, axis=0).T)


def _moba(p3, vt, seq):
    b = p3.shape[0]
    t = MOBA_BLOCK
    n_blocks = seq // t
    n_rows = -(-(n_blocks + 2) // 8) * 8
    return pl.pallas_call(
        functools.partial(_moba_kernel, n_blocks=n_blocks),
        grid=(b, 2, n_blocks),
        in_specs=[
            pl.BlockSpec((None, t, LANES), lambda bi, s, qi: (bi, qi, P_MOBA_Q + s)),
            pl.BlockSpec((None, seq, LANES), lambda bi, s, qi: (bi, 0, P_MOBA_K + s)),
            pl.BlockSpec((None, 2, n_blocks, LANES, t), lambda bi, s, qi: (bi, VT_MOBA // 2 + s, 0, 0, 0)),
            pl.BlockSpec((2, t, t), lambda bi, s, qi: (0, 0, 0)),
        ],
        out_specs=pl.BlockSpec((None, t, LANES), lambda bi, s, qi: (bi, qi, s)),
        out_shape=jax.ShapeDtypeStruct((b, seq, 2 * LANES), ACT_DTYPE),
        scratch_shapes=[
            pltpu.VMEM((n_rows, LANES), jnp.float32),
            pltpu.VMEM((2, n_rows, t), jnp.float32),
            pltpu.VMEM((2, LANES, t), jnp.float32),
            pltpu.VMEM((2, 2, t, t), jnp.float32),
            pltpu.VMEM((2, 2, t, t), ACT_DTYPE),
        ],
        compiler_params=pltpu.CompilerParams(
            dimension_semantics=("parallel", "parallel", "arbitrary"), vmem_limit_bytes=VMEM_LIMIT),
        name="moba",
    )(p3, p3, vt, _causal_bias_tiles(t, t))


def _diff_kernel(lam_ref, q_ref, k_ref, vt_ref, sg_ref, bias_ref, o_ref, acc_ref, s_ref, p_ref, *, tq, lam_init):
    tk = VT_TILE
    qi = pl.program_id(2)
    per_q = tq // tk
    first_diag = qi * per_q
    last = first_diag + per_q - 1
    qs = _split_heads_in_slab(q_ref[...])

    def score_tile(j, c, masked):
        kj = k_ref[pl.ds(pl.multiple_of(jnp.minimum(j, last) * tk, tk), tk), :]
        s = lax.dot_general(kj, qs[c], _NT, preferred_element_type=jnp.float32)
        return s + bias_ref[jnp.clip(j - first_diag + 1, 0, per_q)] if masked else s

    def query_keep(j, c):
        return jnp.broadcast_to(j <= last, (1, tq))

    def value_tile(j, c):
        return vt_ref[jnp.clip(j, 0, last)]

    for c in range(2):
        acc_ref[c] = jnp.zeros(acc_ref.shape[1:], jnp.float32)
    _attend_pipelined(last + 1, first_diag, 2, score_tile, query_keep, value_tile, s_ref, p_ref, acc_ref)

    lp = lam_ref[...]
    lam = (jnp.exp(jnp.sum(lp[0:1, :] * lp[1:2, :], axis=-1, keepdims=True))
           - jnp.exp(jnp.sum(lp[2:3, :] * lp[3:4, :], axis=-1, keepdims=True)) + lam_init)
    o1, o2 = (acc_ref[c, 0:LANES, :] / acc_ref[c, LANES:LANES + 1, :] for c in range(2))
    a = (o1 - lam * o2).T
    ms = jnp.mean(a * a, axis=-1, keepdims=True)
    o_ref[...] = _bf16(a * lax.rsqrt(ms + EPS) * sg_ref[...] * (1.0 - lam_init))


def _diff(p3, vt, lam_params, subln_g, seq, lam_init, tq):
    b = p3.shape[0]
    n_heads = N_DIFF_HEADS
    tk = VT_TILE
    bias = _causal_bias_tiles(tk, tq)
    return pl.pallas_call(
        functools.partial(_diff_kernel, tq=tq, lam_init=lam_init),
        grid=(b, n_heads, seq // tq),
        in_specs=[
            pl.BlockSpec((4, HEAD_DIM), lambda bi, h, qi: (0, 0)),
            pl.BlockSpec((None, tq, LANES), lambda bi, h, qi: (bi, qi, P_DIFF_Q + h)),
            pl.BlockSpec((None, seq, LANES), lambda bi, h, qi: (bi, 0, P_DIFF_K + h)),
            pl.BlockSpec((None, None, seq // tk, VT_DIFF_ROWS, tk), lambda bi, h, qi: (bi, h, 0, 0, 0)),
            pl.BlockSpec((1, LANES), lambda bi, h, qi: (0, 0)),
            pl.BlockSpec(bias.shape, lambda bi, h, qi: (0, 0, 0)),
        ],
        out_specs=pl.BlockSpec((None, tq, LANES), lambda bi, h, qi: (bi, qi, h)),
        out_shape=jax.ShapeDtypeStruct((b, seq, n_heads * LANES), ACT_DTYPE),
        scratch_shapes=[pltpu.VMEM((2, VT_DIFF_ROWS, tq), jnp.float32),
                        pltpu.VMEM((2, 2, tk, tq), jnp.float32),
                        pltpu.VMEM((2, 2, tk, tq), ACT_DTYPE)],
        compiler_params=pltpu.CompilerParams(
            dimension_semantics=("parallel", "parallel", "parallel"), vmem_limit_bytes=VMEM_LIMIT),
        name="diff",
    )(lam_params, p3, p3, vt, subln_g, bias)


def _key_to_f32(key):
    bits = key ^ ((key >> 31) & jnp.int32(0x7FFFFFFF))
    return lax.bitcast_convert_type(bits, jnp.float32)


def _dsa_kernel(iq_ref, ik_ref, iw_ref, q_ref, k_ref, vt_ref, o_ref,
                score_ref, acc_ref, j_ref, s_ref, p_ref, *, tq, n_top, idx_bits):
    t = VT_TILE
    qi = pl.program_id(1)
    per_q = tq // t
    first_diag = qi * per_q
    n_chunks = first_diag + per_q
    last = n_chunks - 1
    key_in_tile = lax.broadcasted_iota(jnp.int32, (t, tq), 0)
    qry_in_tile = lax.broadcasted_iota(jnp.int32, (t, tq), 1)

    iw_t = iw_ref[...].T
    iq_heads = []
    for slab in range(IDX_HEADS // 2):
        iq_heads.extend(_split_heads_in_slab(iq_ref[:, slab * LANES:(slab + 1) * LANES]))

    def index_chunk(c, diag):
        kk = ik_ref[pl.ds(pl.multiple_of(c * t, t), t), :]
        sc = jnp.zeros((t, tq), jnp.float32)
        for h in range(IDX_HEADS):
            lg = lax.dot_general(kk, iq_heads[h], _NT, preferred_element_type=jnp.float32)
            sc = sc + jnp.maximum(lg, 0.0) * iw_t[h:h + 1, :]
        if diag is not None:
            sc = jnp.where(key_in_tile + diag * t <= qry_in_tile, sc, NEG_INF)
        score_ref[c] = sc

    def index_body(c, carry):
        index_chunk(c, None)
        return carry

    lax.fori_loop(0, first_diag, index_body, 0)
    for r in range(per_q):
        index_chunk(first_diag + r, r)

    if per_q % 2 == 1:
        @pl.when(n_chunks % 2 == 1)
        def _():
            score_ref[n_chunks] = jnp.full((t, tq), NEG_INF, jnp.float32)

    n_pairs = (n_chunks + 1) // 2

    def count_keys(hits):
        def body(i, acc):
            for c in (2 * i, 2 * i + 1):
                acc = acc + jnp.sum(hits(score_ref[c], c).reshape(t // 8, 8, tq), axis=0)
            return acc
        acc = lax.fori_loop(0, n_pairs, body, jnp.zeros((8, tq), jnp.float32))
        return jnp.sum(acc, axis=0, keepdims=True)

    def count_ge(cand_key):
        cand = _key_to_f32(cand_key)
        return count_keys(lambda s, c: jnp.where(s >= cand, 1.0, 0.0))

    k_top = jnp.float32(n_top)
    lowest = _key_to_f32(jnp.int32(LOWEST_KEY))

    def survey(c, carry):
        mx, n_fin, n_ge0, n_gt0 = carry
        s = score_ref[c]
        fold = lambda x: x.reshape(t // 8, 8, tq)
        return (jnp.maximum(mx, jnp.max(fold(s), axis=0)),
                n_fin + jnp.sum(fold(jnp.where(s >= lowest, 1.0, 0.0)), axis=0),
                n_ge0 + jnp.sum(fold(jnp.where(s >= 0.0, 1.0, 0.0)), axis=0),
                n_gt0 + jnp.sum(fold(jnp.where(s > 0.0, 1.0, 0.0)), axis=0))

    zeros8 = jnp.zeros((8, tq), jnp.float32)
    mx, n_fin, n_ge0, n_gt0 = lax.fori_loop(
        0, n_chunks, survey, (jnp.full((8, tq), NEG_INF, jnp.float32), zeros8, zeros8, zeros8))
    mx = jnp.max(mx, axis=0, keepdims=True)
    n_fin, n_ge0, n_gt0 = (jnp.sum(v, axis=0, keepdims=True) for v in (n_fin, n_ge0, n_gt0))

    mx_bits = lax.bitcast_convert_type(mx, jnp.int32)
    mx_key = mx_bits ^ ((mx_bits >> 31) & jnp.int32(0x7FFFFFFF))
    few = n_fin <= k_top
    at_zero = n_ge0 >= k_top
    zero_key = jnp.zeros((1, tq), jnp.int32)
    lo = jnp.where(few, LOWEST_KEY, jnp.where(at_zero, zero_key, LOWEST_KEY))
    c_lo = jnp.where(few, n_fin, jnp.where(at_zero, n_ge0, n_fin))
    hi = jnp.where(at_zero, jnp.where(n_gt0 < k_top, zero_key + 1, mx_key + 1), zero_key)
    hi = jnp.where(few, lo + 1, hi)

    def still_active(lo, hi, c_lo):
        return jnp.where(c_lo == k_top, 0.0, jnp.where(hi - lo <= 1, 0.0, 1.0))

    def search_cond(state):
        it, n_active = state[0], state[1]
        return jnp.logical_and(n_active > 0.0, it < 34)

    def search_step(state):
        it, _, lo, hi, c_lo = state
        for _ in range(SEARCH_STEPS_PER_CHECK):
            active = still_active(lo, hi, c_lo) > 0.0
            cand = lo + ((hi - lo) >> 1)
            cnt = count_ge(cand)
            ge = cnt >= k_top
            lo, hi, c_lo = (jnp.where(active, jnp.where(ge, cand, lo), lo),
                            jnp.where(active, jnp.where(ge, hi, cand), hi),
                            jnp.where(active, jnp.where(ge, cnt, c_lo), c_lo))
        return it + SEARCH_STEPS_PER_CHECK, jnp.sum(still_active(lo, hi, c_lo)), lo, hi, c_lo

    _, _, lo, hi, c_lo = lax.while_loop(
        search_cond, search_step, (jnp.int32(0), jnp.sum(still_active(lo, hi, c_lo)), lo, hi, c_lo))
    thr = _key_to_f32(lo)

    j_ref[...] = jnp.full(j_ref.shape, 2 ** 30, jnp.int32)

    @pl.when(jnp.max(jnp.where(c_lo > k_top, 1.0, 0.0)) > 0.0)
    def _():
        need = k_top - count_keys(lambda s, c: jnp.where(s > thr, 1.0, 0.0))

        def jbisect(it, lo):
            cand = lo + (jnp.int32(1) << (idx_bits - 1 - it))
            cnt = count_keys(lambda s, c: jnp.where(
                s == thr, jnp.where((key_in_tile + c * t) <= cand, 1.0, 0.0), 0.0))
            return jnp.where(cnt < need, cand, lo)
        lo = lax.fori_loop(0, idx_bits, jbisect, jnp.full((1, tq), -1, jnp.int32))
        j_ref[...] = jnp.broadcast_to(lo + 1, j_ref.shape)

    j_last = j_ref[0:1, :]

    n_heads = 4
    qhs = []
    for slab in range(n_heads // 2):
        qhs.extend(_split_heads_in_slab(q_ref[:, slab * LANES:(slab + 1) * LANES]))
    for hd in range(n_heads):
        acc_ref[hd] = jnp.zeros((LANES, tq), jnp.float32)

    shared = {}

    def score_tile(j, hd, masked):
        jc = jnp.minimum(j, last)
        if hd == 0:
            s_idx = score_ref[jc]
            keep_tie = jnp.where((key_in_tile + jc * t) <= j_last, 0.0, NEG_INF)
            shared["bias"] = jnp.where(s_idx > thr, 0.0, jnp.where(s_idx == thr, keep_tie, NEG_INF))
        slab = hd // 2
        kj = k_ref[pl.ds(pl.multiple_of(jc * t, t), t), slab * LANES:(slab + 1) * LANES]
        return lax.dot_general(kj, qhs[hd], _NT, preferred_element_type=jnp.float32) + shared["bias"]

    def query_keep(j, hd):
        return jnp.broadcast_to(j <= last, (1, tq))

    def value_tile(j, hd):
        return vt_ref[hd, jnp.clip(j, 0, last)]

    _attend_pipelined(n_chunks, None, n_heads, score_tile, query_keep, value_tile, s_ref, p_ref, acc_ref)
    outs = [acc_ref[hd, 0:HEAD_DIM, :] / acc_ref[hd, HEAD_DIM:HEAD_DIM + 1, :] for hd in range(n_heads)]
    for slab in range(n_heads // 2):
        o_ref[:, slab * LANES:(slab + 1) * LANES] = _bf16(jnp.concatenate(outs[2 * slab:2 * slab + 2], axis=0).T)


def _dsa(p3, vt, iw3, seq, tq):
    b = p3.shape[0]
    t = VT_TILE
    n_top = min(DSA_TOPK_MAX, seq // 4)
    idx_bits = max(1, (seq - 1).bit_length())
    n_tiles = seq // t
    n_score_tiles = n_tiles + (tq // t) % 2
    return pl.pallas_call(
        functools.partial(_dsa_kernel, tq=tq, n_top=n_top, idx_bits=idx_bits),
        grid=(b, seq // tq),
        in_specs=[
            pl.BlockSpec((None, tq, 4 * LANES), lambda bi, qi: (bi, qi, P_IQ // 4)),
            pl.BlockSpec((None, seq, LANES), lambda bi, qi: (bi, 0, P_IK)),
            pl.BlockSpec((None, tq, LANES), lambda bi, qi: (bi, qi, 0)),
            pl.BlockSpec((None, tq, 2 * LANES), lambda bi, qi: (bi, qi, P_DSA_Q // 2)),
            pl.BlockSpec((None, seq, 2 * LANES), lambda bi, qi: (bi, 0, P_DSA_K // 2)),
            pl.BlockSpec((None, 4, n_tiles, LANES, t), lambda bi, qi: (bi, VT_DSA // 4, 0, 0, 0)),
        ],
        out_specs=pl.BlockSpec((None, tq, 2 * LANES), lambda bi, qi: (bi, qi, 0)),
        out_shape=jax.ShapeDtypeStruct((b, seq, 2 * LANES), ACT_DTYPE),
        scratch_shapes=[
            pltpu.VMEM((n_score_tiles, t, tq), jnp.float32),
            pltpu.VMEM((4, LANES, tq), jnp.float32),
            pltpu.VMEM((8, tq), jnp.int32),
            pltpu.VMEM((2, 4, t, tq), jnp.float32),
            pltpu.VMEM((2, 4, t, tq), ACT_DTYPE),
        ],
        compiler_params=pltpu.CompilerParams(
            dimension_semantics=("parallel", "parallel"), vmem_limit_bytes=VMEM_LIMIT),
        name="dsa",
    )(p3, p3, iw3, p3, p3, vt)


def _ffn_kernel(x_ref, moba_ref, diff_ref, dsa_ref, wo_ref, g_ref, wg_ref, wu_ref, wd_ref, o_ref, *, ff_chunk):
    w_moba, w_diff = 2 * LANES, 4 * LANES
    y = (x_ref[...]
         + jnp.dot(moba_ref[...], wo_ref[0:w_moba, :], preferred_element_type=jnp.float32)
         + jnp.dot(diff_ref[...], wo_ref[w_moba:w_moba + w_diff, :], preferred_element_type=jnp.float32)
         + jnp.dot(dsa_ref[...], wo_ref[w_moba + w_diff:, :], preferred_element_type=jnp.float32))
    ms = jnp.mean(y * y, axis=-1, keepdims=True)
    h = _bf16(y * lax.rsqrt(ms + EPS) * g_ref[...])
    o_ref[...] = y
    for c in range(D_FF // ff_chunk):
        sl = slice(c * ff_chunk, (c + 1) * ff_chunk)
        gate = jnp.dot(h, wg_ref[:, sl], preferred_element_type=jnp.float32)
        up = jnp.dot(h, wu_ref[:, sl], preferred_element_type=jnp.float32)
        act = _bf16(gate * (1.0 / (1.0 + jnp.exp(-gate))) * up)
        o_ref[...] += jnp.dot(act, wd_ref[sl, :], preferred_element_type=jnp.float32)


def _ffn(x2d, moba, diff, dsa, wo, g, wg, wu, wd, tm, ff_chunk):
    m = x2d.shape[0]
    const = lambda i: (0, 0)
    resident = dict(pipeline_mode=pl.Buffered(1))
    return pl.pallas_call(
        functools.partial(_ffn_kernel, ff_chunk=ff_chunk),
        grid=(m // tm,),
        in_specs=[
            pl.BlockSpec((tm, D_MODEL), lambda i: (i, 0)),
            pl.BlockSpec((tm, 2 * LANES), lambda i: (i, 0)),
            pl.BlockSpec((tm, 4 * LANES), lambda i: (i, 0)),
            pl.BlockSpec((tm, 2 * LANES), lambda i: (i, 0)),
            pl.BlockSpec((D_MODEL, D_MODEL), const, **resident),
            pl.BlockSpec((1, D_MODEL), const),
            pl.BlockSpec((D_MODEL, D_FF), const, **resident),
            pl.BlockSpec((D_MODEL, D_FF), const, **resident),
            pl.BlockSpec((D_FF, D_MODEL), const, **resident),
        ],
        out_specs=pl.BlockSpec((tm, D_MODEL), lambda i: (i, 0)),
        out_shape=jax.ShapeDtypeStruct((m, D_MODEL), jnp.float32),
        compiler_params=pltpu.CompilerParams(
            dimension_semantics=("parallel",), vmem_limit_bytes=VMEM_LIMIT),
        name="outproj_ffn",
    )(x2d, moba, diff, dsa, wo, g, wg, wu, wd)


def _rope_tables(seq):
    inv = 1.0 / (ROPE_THETA ** (jnp.arange(0, HEAD_DIM, 2, dtype=jnp.float32) / HEAD_DIM))
    ang = jnp.arange(seq, dtype=jnp.float32)[:, None] * inv[None, :]
    cos, sin = jnp.cos(ang), jnp.sin(ang)
    return jnp.tile(cos, (1, 4)), jnp.concatenate([-sin, sin, -sin, sin], axis=1)


def _pack_w_in(w):
    ik = w[:, SLAB_IK * LANES:SLAB_IK * LANES + HEAD_DIM]
    iw = w[:, SLAB_IK * LANES + HEAD_DIM:IN_COLS]
    pad = jnp.zeros((w.shape[0], LANES - IDX_HEADS), w.dtype)
    return _bf16(jnp.concatenate([w[:, :SLAB_IK * LANES], ik, ik, iw, pad], axis=1))


def kernel(x, attn_norm_g, w_in, q_norm_g, k_norm_g, diff_lambda, diff_subln_g,
           w_out, ffn_norm_g, w_gate, w_up, w_down):
    b, seq, d = x.shape
    depth = w_in.shape[0]
    assert d == D_MODEL and seq % ROW_TILE == 0 and MOBA_BLOCK == VT_TILE
    tq = ROW_TILE
    tm_in = ROW_TILE
    tm_ffn = ROW_TILE
    cos_t, sin_t = _rope_tables(seq)
    x2d = x.reshape(b * seq, d)

    for l in range(depth):
        gains = jnp.stack([q_norm_g[l, 0], k_norm_g[l, 0], q_norm_g[l, 1], k_norm_g[l, 1],
                           q_norm_g[l, 2], k_norm_g[l, 2]])
        head_gains = jnp.concatenate(
            [jnp.tile(gains, (1, 2)), jnp.ones((2, LANES), jnp.float32)], axis=0)
        p, vt, vtd, iw = _inproj(x2d, attn_norm_g[l][None, :], _pack_w_in(w_in[l]), cos_t, sin_t,
                                 head_gains, seq, tm_in)
        p3 = p.reshape(b, seq, N_P_SLABS * LANES)
        iw3 = iw.reshape(b, seq, LANES)

        moba = _moba(p3, vt, seq)
        lam_init = 0.8 - 0.6 * math.exp(-0.3 * l)
        diff = _diff(p3, vtd, diff_lambda[l], diff_subln_g[l][None, :], seq, lam_init, tq)
        dsa = _dsa(p3, vt, iw3, seq, tq)

        x2d = _ffn(x2d, moba.reshape(b * seq, -1), diff.reshape(b * seq, -1), dsa.reshape(b * seq, -1),
                   _bf16(w_out[l]), ffn_norm_g[l][None, :], _bf16(w_gate[l]), _bf16(w_up[l]),
                   _bf16(w_down[l]), tm_ffn, 704)
    return x2d.reshape(b, seq, d)
```

```python
import functools
import math

import jax
import jax.numpy as jnp
from jax import lax
from jax.experimental import pallas as pl
from jax.experimental.pallas import tpu as pltpu

D_MODEL = 1024
HEAD_DIM = 64
MOBA_BLOCK = 256
MOBA_TOPK = 3
DSA_TOPK_MAX = 256
IDX_HEADS = 8
ROPE_THETA = 10000.0
EPS = 1e-6
D_FF = 2816

LANES = 128
HALF = HEAD_DIM // 2

SLAB_MOBA_Q, SLAB_MOBA_K, SLAB_MOBA_V = 0, 2, 4
SLAB_DIFF_Q, SLAB_DIFF_K, SLAB_DIFF_V = 6, 10, 14
SLAB_DSA_Q, SLAB_DSA_K, SLAB_DSA_V = 18, 20, 22
SLAB_IQ, SLAB_IK, SLAB_IW = 24, 28, 29
N_SLABS = 30
IN_COLS = 3656
SLABS_PER_DOT = 4

P_MOBA_Q, P_MOBA_K, P_DIFF_Q, P_DIFF_K, P_DSA_Q, P_DSA_K, P_IQ, P_IK = 0, 2, 4, 8, 12, 14, 16, 20
N_P_SLABS = 21
VT_MOBA, VT_DSA = 0, 4
N_VT_HEADS = 8
N_DIFF_HEADS = 4
VT_DIFF_ROWS = LANES + 16
VT_TILE = 256
ROW_TILE = 512

ATTN_SCALE = HEAD_DIM ** -0.5 * math.log2(math.e)
IDX_SCALE = 64 ** -0.5

VMEM_LIMIT = 56 * 1024 * 1024
ACT_DTYPE = jnp.bfloat16

NEG_INF = float("-inf")
SEARCH_STEPS_PER_CHECK = 4
LOWEST_KEY = -2139095040


def _f32(x):
    return x.astype(jnp.float32)


def _bf16(x):
    return x.astype(ACT_DTYPE)


def _slab_kind(s):
    if s < SLAB_MOBA_K:
        return True, 0, True, ATTN_SCALE, ("p", P_MOBA_Q + s - SLAB_MOBA_Q)
    if s < SLAB_MOBA_V:
        return True, 1, True, 1.0, ("p", P_MOBA_K + s - SLAB_MOBA_K)
    if s < SLAB_DIFF_Q:
        return False, 0, False, 1.0, ("vt", VT_MOBA + 2 * (s - SLAB_MOBA_V))
    if s < SLAB_DIFF_K:
        return True, 2, True, ATTN_SCALE, ("p", P_DIFF_Q + s - SLAB_DIFF_Q)
    if s < SLAB_DIFF_V:
        return True, 3, True, 1.0, ("p", P_DIFF_K + s - SLAB_DIFF_K)
    if s < SLAB_DSA_Q:
        return False, 0, False, 1.0, ("vtd", s - SLAB_DIFF_V)
    if s < SLAB_DSA_K:
        return True, 4, True, ATTN_SCALE, ("p", P_DSA_Q + s - SLAB_DSA_Q)
    if s < SLAB_DSA_V:
        return True, 5, True, 1.0, ("p", P_DSA_K + s - SLAB_DSA_K)
    if s < SLAB_IQ:
        return False, 0, False, 1.0, ("vt", VT_DSA + 2 * (s - SLAB_DSA_V))
    if s < SLAB_IK:
        return False, 0, True, IDX_SCALE, ("p", P_IQ + s - SLAB_IQ)
    if s < SLAB_IW:
        return False, 0, True, 1.0, ("p", P_IK)
    return False, 0, False, IDX_HEADS ** -0.5, ("iw",)


def _inproj_kernel(x_ref, g_ref, w_ref, cos_ref, sin_ref, hg_ref, p_ref, vt_ref, vtd_ref, iw_ref):
    x = x_ref[...]
    ms = jnp.mean(x * x, axis=-1, keepdims=True)
    h = _bf16(x * lax.rsqrt(ms + EPS) * g_ref[...])

    lane = lax.broadcasted_iota(jnp.int32, (1, LANES), 1)
    first_half = (lane & (HEAD_DIM - 1)) < HALF
    r = lax.broadcasted_iota(jnp.int32, (LANES, LANES), 0) // HEAD_DIM
    c = lax.broadcasted_iota(jnp.int32, (LANES, LANES), 1) // HEAD_DIM
    head_ones = _bf16(jnp.where(r == c, 1.0, 0.0))
    cos = cos_ref[...]
    sin = sin_ref[...]
    hg = hg_ref[...]

    tm = x.shape[0]
    ones_rows = jnp.ones((HEAD_DIM, VT_TILE), vt_ref.dtype)
    for g0 in range(0, N_SLABS, SLABS_PER_DOT):
        n = min(SLABS_PER_DOT, N_SLABS - g0)
        yg = jnp.dot(h, w_ref[:, g0 * LANES:(g0 + n) * LANES], preferred_element_type=jnp.float32)
        for k in range(n):
            s = g0 + k
            y = yg[:, k * LANES:(k + 1) * LANES]
            norm, grow, rope, scale, dest = _slab_kind(s)
            if norm:
                yy = y * y
                hi = _bf16(yy)
                lo = _bf16(yy - _f32(hi))
                ss = (jnp.dot(hi, head_ones, preferred_element_type=jnp.float32)
                      + jnp.dot(lo, head_ones, preferred_element_type=jnp.float32))
                y = y * lax.rsqrt(ss * (1.0 / HEAD_DIM) + EPS) * hg[grow:grow + 1, :]
            if rope:
                partner = jnp.where(first_half, pltpu.roll(y, LANES - HALF, 1), pltpu.roll(y, HALF, 1))
                y = y * cos + partner * sin
            if scale != 1.0:
                y = y * scale
            if dest[0] == "p":
                p_ref[:, dest[1] * LANES:(dest[1] + 1) * LANES] = _bf16(y)
            elif dest[0] == "iw":
                iw_ref[...] = y
            else:
                head = dest[1]
                for r in range(tm // VT_TILE):
                    y_t = _bf16(y[r * VT_TILE:(r + 1) * VT_TILE, :].T)
                    if dest[0] == "vtd":
                        vtd_ref[head, r, 0:LANES, :] = y_t
                        vtd_ref[head, r, LANES:, :] = ones_rows[0:VT_DIFF_ROWS - LANES, :]
                    else:
                        for hh in range(2):
                            vt_ref[head + hh, r, 0:HEAD_DIM, :] = y_t[hh * HEAD_DIM:(hh + 1) * HEAD_DIM, :]
                            vt_ref[head + hh, r, HEAD_DIM:, :] = ones_rows


def _inproj(x2d, g, w_packed, cos_t, sin_t, head_gains, seq, tm):
    m = x2d.shape[0]
    nc = w_packed.shape[1]
    n_seq_tiles = seq // tm
    return pl.pallas_call(
        _inproj_kernel,
        grid=(m // tm,),
        in_specs=[
            pl.BlockSpec((tm, D_MODEL), lambda i: (i, 0)),
            pl.BlockSpec((1, D_MODEL), lambda i: (0, 0)),
            pl.BlockSpec((D_MODEL, nc), lambda i: (0, 0)),
            pl.BlockSpec((tm, LANES), lambda i: (i % n_seq_tiles, 0)),
            pl.BlockSpec((tm, LANES), lambda i: (i % n_seq_tiles, 0)),
            pl.BlockSpec((8, LANES), lambda i: (0, 0)),
        ],
        out_specs=[
            pl.BlockSpec((tm, N_P_SLABS * LANES), lambda i: (i, 0)),
            pl.BlockSpec((None, N_VT_HEADS, tm // VT_TILE, LANES, VT_TILE),
                         lambda i: (i // n_seq_tiles, 0, i % n_seq_tiles, 0, 0)),
            pl.BlockSpec((None, N_DIFF_HEADS, tm // VT_TILE, VT_DIFF_ROWS, VT_TILE),
                         lambda i: (i // n_seq_tiles, 0, i % n_seq_tiles, 0, 0)),
            pl.BlockSpec((tm, LANES), lambda i: (i, 0)),
        ],
        out_shape=[
            jax.ShapeDtypeStruct((m, N_P_SLABS * LANES), ACT_DTYPE),
            jax.ShapeDtypeStruct((m // seq, N_VT_HEADS, seq // VT_TILE, LANES, VT_TILE), ACT_DTYPE),
            jax.ShapeDtypeStruct((m // seq, N_DIFF_HEADS, seq // VT_TILE, VT_DIFF_ROWS, VT_TILE), ACT_DTYPE),
            jax.ShapeDtypeStruct((m, LANES), jnp.float32),
        ],
        compiler_params=pltpu.CompilerParams(
            dimension_semantics=("parallel",), vmem_limit_bytes=VMEM_LIMIT),
        name="inproj",
    )(x2d, g, w_packed, cos_t, sin_t, head_gains)


def _attend_pipelined(n_tiles, first_masked, n_maps, score_tile, query_keep, value_tile,
                      s_ref, p_ref, acc_ref):
    maps = range(n_maps)

    def put_scores(j, slot, masked):
        smax = []
        for c in maps:
            s = score_tile(j, c, masked)
            s_ref[slot, c] = s
            smax.append(jnp.where(query_keep(j, c), jnp.max(s, axis=0, keepdims=True), NEG_INF))
        return tuple(smax)

    def softmax(j, slot, smax, m_old):
        alpha, m_new = [], []
        for c in maps:
            m = jnp.maximum(m_old[c], smax[c])
            m_safe = jnp.where(m == NEG_INF, 0.0, m)
            alpha.append(jnp.exp2(m_old[c] - m_safe))
            p = jnp.exp2(s_ref[slot, c] - jnp.where(query_keep(j, c), m_safe, jnp.inf))
            p_ref[slot, c] = _bf16(p)
            m_new.append(m)
        return tuple(alpha), tuple(m_new)

    def add_values(j, slot, alpha):
        for c in maps:
            acc_ref[c] = alpha[c] * acc_ref[c] + jnp.dot(value_tile(j, c), p_ref[slot, c],
                                                         preferred_element_type=jnp.float32)

    def trip(masked, i, carry):
        smax0, alpha_prev, m = carry
        j = 2 * i
        smax1 = put_scores(j + 1, 1, masked)
        alpha0, m = softmax(j, 0, smax0, m)
        add_values(j - 1, 1, alpha_prev)
        smax0 = put_scores(j + 2, 0, masked)
        alpha1, m = softmax(j + 1, 1, smax1, m)
        add_values(j, 0, alpha0)
        return smax0, alpha1, m

    tk, tq = p_ref.shape[2:]
    m0 = (jnp.full((1, tq), NEG_INF, jnp.float32),) * n_maps
    one = (jnp.ones((1, tq), jnp.float32),) * n_maps
    for c in maps:
        p_ref[1, c] = jnp.zeros((tk, tq), p_ref.dtype)
    n_trips = (n_tiles + 1) // 2
    carry = (put_scores(0, 0, True), one, m0)
    n_plain = 0
    if first_masked is not None:
        n_plain = jnp.clip((first_masked - 1) // 2, 0, n_trips)
        carry = lax.fori_loop(0, n_plain, functools.partial(trip, False), carry)
    _, alpha_last, _ = lax.fori_loop(n_plain, n_trips, functools.partial(trip, True), carry)
    add_values(2 * n_trips - 1, 1, alpha_last)


def _causal_bias_tiles(tk, tq):
    key = lax.broadcasted_iota(jnp.int32, (tk, tq), 0)
    qry = lax.broadcasted_iota(jnp.int32, (tk, tq), 1)
    tiles = [jnp.zeros((tk, tq), jnp.float32)]
    for r in range(tq // tk):
        tiles.append(jnp.where(key + r * tk <= qry, 0.0, NEG_INF).astype(jnp.float32))
    return jnp.stack(tiles)


def _split_heads_in_slab(q):
    q_t = _f32(q).T
    row = lax.broadcasted_iota(jnp.int32, (LANES, 1), 0)
    return _bf16(jnp.where(row < HEAD_DIM, q_t, 0.0)), _bf16(jnp.where(row >= HEAD_DIM, q_t, 0.0))


def _moba_kernel(q_ref, k_ref, vt_ref, bias_ref, o_ref, kmean_ref, sel_ref, acc_ref, s_ref, p_ref, *, n_blocks):
    t = MOBA_BLOCK
    qi = pl.program_id(2)

    @pl.when(qi == 0)
    def _():
        kmean_ref[...] = jnp.zeros_like(kmean_ref)
        for j in range(n_blocks):
            kb = _f32(k_ref[j * t:(j + 1) * t, :])
            kmean_ref[j:j + 1, :] = jnp.sum(kb, axis=0, keepdims=True) * (1.0 / t)

    km = kmean_ref[...]
    km_hi = _bf16(km)
    km_lo = _bf16(km - _f32(km_hi))
    n_rows = kmean_ref.shape[0]
    blk = lax.broadcasted_iota(jnp.int32, (n_rows, t), 0)
    blk_f = _f32(blk)
    n_sel = min(MOBA_TOPK, n_blocks - 1)
    qhs = _split_heads_in_slab(q_ref[...])

    for hd, qh in enumerate(qhs):
        gate = (jnp.dot(km_hi, qh, preferred_element_type=jnp.float32)
                + jnp.dot(km_lo, qh, preferred_element_type=jnp.float32))
        gate = jnp.where(blk < qi, gate, NEG_INF)
        sel = jnp.zeros((n_rows, t), jnp.float32)
        for rank in range(n_sel):
            gmax = jnp.max(gate, axis=0, keepdims=True)
            first = jnp.min(jnp.where(gate == gmax, blk_f, float(n_rows)), axis=0, keepdims=True)
            pick = blk_f == first
            sel = jnp.where(pick, jnp.maximum(sel, jnp.where(rank < qi, 1.0, 0.0)), sel)
            gate = jnp.where(pick, NEG_INF, gate)
        sel_ref[hd] = jnp.where(blk == qi, 1.0, sel)
        acc_ref[hd] = jnp.zeros((LANES, t), jnp.float32)

    def score_tile(j, c, masked):
        kj = k_ref[pl.ds(pl.multiple_of(jnp.minimum(j, qi) * t, t), t), :]
        s = jnp.dot(kj, qhs[c], preferred_element_type=jnp.float32)
        return s + bias_ref[jnp.where(j >= qi, 1, 0)] if masked else s

    def query_keep(j, c):
        return sel_ref[c, pl.ds(jnp.minimum(j, n_rows - 1), 1), :] > 0.0

    def value_tile(j, c):
        return vt_ref[c, jnp.clip(j, 0, qi)]

    _attend_pipelined(qi + 1, qi, 2, score_tile, query_keep, value_tile, s_ref, p_ref, acc_ref)
    outs = [acc_ref[c, 0:HEAD_DIM, :] / acc_ref[c, HEAD_DIM:HEAD_DIM + 1, :] for c in range(2)]
    o_ref[...] = _bf16(jnp.concatenate(outs, axis=0).T)


def _moba(p3, vt, seq):
    b = p3.shape[0]
    t = MOBA_BLOCK
    n_blocks = seq // t
    n_rows = -(-(n_blocks + 2) // 8) * 8
    return pl.pallas_call(
        functools.partial(_moba_kernel, n_blocks=n_blocks),
        grid=(b, 2, n_blocks),
        in_specs=[
            pl.BlockSpec((None, t, LANES), lambda bi, s, qi: (bi, qi, P_MOBA_Q + s)),
            pl.BlockSpec((None, seq, LANES), lambda bi, s, qi: (bi, 0, P_MOBA_K + s)),
            pl.BlockSpec((None, 2, n_blocks, LANES, t), lambda bi, s, qi: (bi, VT_MOBA // 2 + s, 0, 0, 0)),
            pl.BlockSpec((2, t, t), lambda bi, s, qi: (0, 0, 0)),
        ],
        out_specs=pl.BlockSpec((None, t, LANES), lambda bi, s, qi: (bi, qi, s)),
        out_shape=jax.ShapeDtypeStruct((b, seq, 2 * LANES), ACT_DTYPE),
        scratch_shapes=[
            pltpu.VMEM((n_rows, LANES), jnp.float32),
            pltpu.VMEM((2, n_rows, t), jnp.float32),
            pltpu.VMEM((2, LANES, t), jnp.float32),
            pltpu.VMEM((2, 2, t, t), jnp.float32),
            pltpu.VMEM((2, 2, t, t), ACT_DTYPE),
        ],
        compiler_params=pltpu.CompilerParams(
            dimension_semantics=("parallel", "parallel", "arbitrary"), vmem_limit_bytes=VMEM_LIMIT),
        name="moba",
    )(p3, p3, vt, _causal_bias_tiles(t, t))


def _diff_kernel(lam_ref, q_ref, k_ref, vt_ref, sg_ref, bias_ref, o_ref, acc_ref, s_ref, p_ref, *, tq, lam_init):
    tk = VT_TILE
    qi = pl.program_id(2)
    per_q = tq // tk
    first_diag = qi * per_q
    last = first_diag + per_q - 1
    qs = _split_heads_in_slab(q_ref[...])

    def score_tile(j, c, masked):
        kj = k_ref[pl.ds(pl.multiple_of(jnp.minimum(j, last) * tk, tk), tk), :]
        s = jnp.dot(kj, qs[c], preferred_element_type=jnp.float32)
        return s + bias_ref[jnp.clip(j - first_diag + 1, 0, per_q)] if masked else s

    def query_keep(j, c):
        return jnp.broadcast_to(j <= last, (1, tq))

    def value_tile(j, c):
        return vt_ref[jnp.clip(j, 0, last)]

    for c in range(2):
        acc_ref[c] = jnp.zeros(acc_ref.shape[1:], jnp.float32)
    _attend_pipelined(last + 1, first_diag, 2, score_tile, query_keep, value_tile, s_ref, p_ref, acc_ref)

    lp = lam_ref[...]
    lam = (jnp.exp(jnp.sum(lp[0:1, :] * lp[1:2, :], axis=-1, keepdims=True))
           - jnp.exp(jnp.sum(lp[2:3, :] * lp[3:4, :], axis=-1, keepdims=True)) + lam_init)
    o1, o2 = (acc_ref[c, 0:LANES, :] / acc_ref[c, LANES:LANES + 1, :] for c in range(2))
    a = (o1 - lam * o2).T
    ms = jnp.mean(a * a, axis=-1, keepdims=True)
    o_ref[...] = _bf16(a * lax.rsqrt(ms + EPS) * sg_ref[...] * (1.0 - lam_init))


def _diff(p3, vt, lam_params, subln_g, seq, lam_init, tq):
    b = p3.shape[0]
    n_heads = N_DIFF_HEADS
    tk = VT_TILE
    bias = _causal_bias_tiles(tk, tq)
    return pl.pallas_call(
        functools.partial(_diff_kernel, tq=tq, lam_init=lam_init),
        grid=(b, n_heads, seq // tq),
        in_specs=[
            pl.BlockSpec((4, HEAD_DIM), lambda bi, h, qi: (0, 0)),
            pl.BlockSpec((None, tq, LANES), lambda bi, h, qi: (bi, qi, P_DIFF_Q + h)),
            pl.BlockSpec((None, seq, LANES), lambda bi, h, qi: (bi, 0, P_DIFF_K + h)),
            pl.BlockSpec((None, None, seq // tk, VT_DIFF_ROWS, tk), lambda bi, h, qi: (bi, h, 0, 0, 0)),
            pl.BlockSpec((1, LANES), lambda bi, h, qi: (0, 0)),
            pl.BlockSpec(bias.shape, lambda bi, h, qi: (0, 0, 0)),
        ],
        out_specs=pl.BlockSpec((None, tq, LANES), lambda bi, h, qi: (bi, qi, h)),
        out_shape=jax.ShapeDtypeStruct((b, seq, n_heads * LANES), ACT_DTYPE),
        scratch_shapes=[pltpu.VMEM((2, VT_DIFF_ROWS, tq), jnp.float32),
                        pltpu.VMEM((2, 2, tk, tq), jnp.float32),
                        pltpu.VMEM((2, 2, tk, tq), ACT_DTYPE)],
        compiler_params=pltpu.CompilerParams(
            dimension_semantics=("parallel", "parallel", "parallel"), vmem_limit_bytes=VMEM_LIMIT),
        name="diff",
    )(lam_params, p3, p3, vt, subln_g, bias)


def _key_to_f32(key):
    bits = key ^ ((key >> 31) & jnp.int32(0x7FFFFFFF))
    return lax.bitcast_convert_type(bits, jnp.float32)


def _dsa_kernel(iq_ref, ik_ref, iw_ref, q_ref, k_ref, vt_ref, o_ref,
                score_ref, acc_ref, j_ref, s_ref, p_ref, *, tq, n_top, idx_bits):
    t = VT_TILE
    qi = pl.program_id(1)
    per_q = tq // t
    first_diag = qi * per_q
    n_chunks = first_diag + per_q
    last = n_chunks - 1
    key_in_tile = lax.broadcasted_iota(jnp.int32, (t, tq), 0)
    qry_in_tile = lax.broadcasted_iota(jnp.int32, (t, tq), 1)

    iw_t = iw_ref[...].T
    iq_heads = []
    for slab in range(IDX_HEADS // 2):
        iq_heads.extend(_split_heads_in_slab(iq_ref[:, slab * LANES:(slab + 1) * LANES]))

    k_top = jnp.float32(n_top)
    lowest = _key_to_f32(jnp.int32(LOWEST_KEY))

    def index_chunk(c, diag, survey):
        kk = ik_ref[pl.ds(pl.multiple_of(c * t, t), t), :]
        sc = jnp.zeros((t, tq), jnp.float32)
        for h in range(IDX_HEADS):
            lg = jnp.dot(kk, iq_heads[h], preferred_element_type=jnp.float32)
            sc = sc + jnp.maximum(lg, 0.0) * iw_t[h:h + 1, :]
        if diag is not None:
            sc = jnp.where(key_in_tile + diag * t <= qry_in_tile, sc, NEG_INF)
        score_ref[c] = sc
        mx, n_fin, n_ge0, n_gt0 = survey
        fold = lambda x: x.reshape(t // 8, 8, tq)
        return (jnp.maximum(mx, jnp.max(fold(sc), axis=0)),
                n_fin + jnp.sum(fold(jnp.where(sc >= lowest, 1.0, 0.0)), axis=0),
                n_ge0 + jnp.sum(fold(jnp.where(sc >= 0.0, 1.0, 0.0)), axis=0),
                n_gt0 + jnp.sum(fold(jnp.where(sc > 0.0, 1.0, 0.0)), axis=0))

    zeros8 = jnp.zeros((8, tq), jnp.float32)
    survey = lax.fori_loop(0, first_diag, lambda c, sv: index_chunk(c, None, sv),
                           (jnp.full((8, tq), NEG_INF, jnp.float32), zeros8, zeros8, zeros8))
    for r in range(per_q):
        survey = index_chunk(first_diag + r, r, survey)
    mx = jnp.max(survey[0], axis=0, keepdims=True)
    n_fin, n_ge0, n_gt0 = (jnp.sum(v, axis=0, keepdims=True) for v in survey[1:])

    if per_q % 2 == 1:
        @pl.when(n_chunks % 2 == 1)
        def _():
            score_ref[n_chunks] = jnp.full((t, tq), NEG_INF, jnp.float32)

    n_pairs = (n_chunks + 1) // 2

    def count_keys(hits):
        def body(i, acc):
            for c in (2 * i, 2 * i + 1):
                acc = acc + jnp.sum(hits(score_ref[c], c).reshape(t // 8, 8, tq), axis=0)
            return acc
        acc = lax.fori_loop(0, n_pairs, body, jnp.zeros((8, tq), jnp.float32))
        return jnp.sum(acc, axis=0, keepdims=True)

    def count_ge(cand_key):
        cand = _key_to_f32(cand_key)
        return count_keys(lambda s, c: jnp.where(s >= cand, 1.0, 0.0))

    mx_bits = lax.bitcast_convert_type(mx, jnp.int32)
    mx_key = mx_bits ^ ((mx_bits >> 31) & jnp.int32(0x7FFFFFFF))
    few = n_fin <= k_top
    at_zero = n_ge0 >= k_top
    zero_key = jnp.zeros((1, tq), jnp.int32)
    lo = jnp.where(few, LOWEST_KEY, jnp.where(at_zero, zero_key, LOWEST_KEY))
    c_lo = jnp.where(few, n_fin, jnp.where(at_zero, n_ge0, n_fin))
    hi = jnp.where(at_zero, jnp.where(n_gt0 < k_top, zero_key + 1, mx_key + 1), zero_key)
    hi = jnp.where(few, lo + 1, hi)

    def still_active(lo, hi, c_lo):
        return jnp.where(c_lo == k_top, 0.0, jnp.where(hi - lo <= 1, 0.0, 1.0))

    def search_cond(state):
        it, n_active = state[0], state[1]
        return jnp.logical_and(n_active > 0.0, it < 34)

    def search_step(state):
        it, _, lo, hi, c_lo = state
        for _ in range(SEARCH_STEPS_PER_CHECK):
            active = still_active(lo, hi, c_lo) > 0.0
            below = lax.bitcast_convert_type(_key_to_f32(hi - 1) * 0.0625, jnp.int32)
            cand = jnp.where(lo == 0, below, lo + ((hi - lo) >> 1))
            cand = jnp.minimum(jnp.maximum(cand, lo + 1), hi - 1)
            cnt = count_ge(cand)
            ge = cnt >= k_top
            lo, hi, c_lo = (jnp.where(active, jnp.where(ge, cand, lo), lo),
                            jnp.where(active, jnp.where(ge, hi, cand), hi),
                            jnp.where(active, jnp.where(ge, cnt, c_lo), c_lo))
        return it + SEARCH_STEPS_PER_CHECK, jnp.sum(still_active(lo, hi, c_lo)), lo, hi, c_lo

    _, _, lo, hi, c_lo = lax.while_loop(
        search_cond, search_step, (jnp.int32(0), jnp.sum(still_active(lo, hi, c_lo)), lo, hi, c_lo))
    thr = _key_to_f32(lo)

    j_ref[...] = jnp.full(j_ref.shape, 2 ** 30, jnp.int32)

    @pl.when(jnp.max(jnp.where(c_lo > k_top, 1.0, 0.0)) > 0.0)
    def _():
        need = k_top - count_keys(lambda s, c: jnp.where(s > thr, 1.0, 0.0))

        def jbisect(it, lo):
            cand = lo + (jnp.int32(1) << (idx_bits - 1 - it))
            cnt = count_keys(lambda s, c: jnp.where(
                s == thr, jnp.where((key_in_tile + c * t) <= cand, 1.0, 0.0), 0.0))
            return jnp.where(cnt < need, cand, lo)
        lo = lax.fori_loop(0, idx_bits, jbisect, jnp.full((1, tq), -1, jnp.int32))
        j_ref[...] = jnp.broadcast_to(lo + 1, j_ref.shape)

    j_last = j_ref[0:1, :]

    n_heads = 4
    qhs = []
    for slab in range(n_heads // 2):
        qhs.extend(_split_heads_in_slab(q_ref[:, slab * LANES:(slab + 1) * LANES]))
    for hd in range(n_heads):
        acc_ref[hd] = jnp.zeros((LANES, tq), jnp.float32)

    shared = {}

    def score_tile(j, hd, masked):
        jc = jnp.minimum(j, last)
        if hd == 0:
            s_idx = score_ref[jc]
            keep_tie = jnp.where((key_in_tile + jc * t) <= j_last, 0.0, NEG_INF)
            shared["bias"] = jnp.where(s_idx > thr, 0.0, jnp.where(s_idx == thr, keep_tie, NEG_INF))
        slab = hd // 2
        kj = k_ref[pl.ds(pl.multiple_of(jc * t, t), t), slab * LANES:(slab + 1) * LANES]
        return jnp.dot(kj, qhs[hd], preferred_element_type=jnp.float32) + shared["bias"]

    def query_keep(j, hd):
        return jnp.broadcast_to(j <= last, (1, tq))

    def value_tile(j, hd):
        return vt_ref[hd, jnp.clip(j, 0, last)]

    _attend_pipelined(n_chunks, None, n_heads, score_tile, query_keep, value_tile, s_ref, p_ref, acc_ref)
    outs = [acc_ref[hd, 0:HEAD_DIM, :] / acc_ref[hd, HEAD_DIM:HEAD_DIM + 1, :] for hd in range(n_heads)]
    for slab in range(n_heads // 2):
        o_ref[:, slab * LANES:(slab + 1) * LANES] = _bf16(jnp.concatenate(outs[2 * slab:2 * slab + 2], axis=0).T)


def _dsa(p3, vt, iw3, seq, tq):
    b = p3.shape[0]
    t = VT_TILE
    n_top = min(DSA_TOPK_MAX, seq // 4)
    idx_bits = max(1, (seq - 1).bit_length())
    n_tiles = seq // t
    n_score_tiles = n_tiles + (tq // t) % 2
    return pl.pallas_call(
        functools.partial(_dsa_kernel, tq=tq, n_top=n_top, idx_bits=idx_bits),
        grid=(b, seq // tq),
        in_specs=[
            pl.BlockSpec((None, tq, 4 * LANES), lambda bi, qi: (bi, qi, P_IQ // 4)),
            pl.BlockSpec((None, seq, LANES), lambda bi, qi: (bi, 0, P_IK)),
            pl.BlockSpec((None, tq, LANES), lambda bi, qi: (bi, qi, 0)),
            pl.BlockSpec((None, tq, 2 * LANES), lambda bi, qi: (bi, qi, P_DSA_Q // 2)),
            pl.BlockSpec((None, seq, 2 * LANES), lambda bi, qi: (bi, 0, P_DSA_K // 2)),
            pl.BlockSpec((None, 4, n_tiles, LANES, t), lambda bi, qi: (bi, VT_DSA // 4, 0, 0, 0)),
        ],
        out_specs=pl.BlockSpec((None, tq, 2 * LANES), lambda bi, qi: (bi, qi, 0)),
        out_shape=jax.ShapeDtypeStruct((b, seq, 2 * LANES), ACT_DTYPE),
        scratch_shapes=[
            pltpu.VMEM((n_score_tiles, t, tq), jnp.float32),
            pltpu.VMEM((4, LANES, tq), jnp.float32),
            pltpu.VMEM((8, tq), jnp.int32),
            pltpu.VMEM((2, 4, t, tq), jnp.float32),
            pltpu.VMEM((2, 4, t, tq), ACT_DTYPE),
        ],
        compiler_params=pltpu.CompilerParams(
            dimension_semantics=("parallel", "parallel"), vmem_limit_bytes=VMEM_LIMIT),
        name="dsa",
    )(p3, p3, iw3, p3, p3, vt)


def _ffn_kernel(x_ref, moba_ref, diff_ref, dsa_ref, wo_ref, g_ref, wg_ref, wu_ref, wd_ref, o_ref, *, ff_chunk):
    w_moba, w_diff = 2 * LANES, 4 * LANES
    y = (x_ref[...]
         + jnp.dot(moba_ref[...], wo_ref[0:w_moba, :], preferred_element_type=jnp.float32)
         + jnp.dot(diff_ref[...], wo_ref[w_moba:w_moba + w_diff, :], preferred_element_type=jnp.float32)
         + jnp.dot(dsa_ref[...], wo_ref[w_moba + w_diff:, :], preferred_element_type=jnp.float32))
    ms = jnp.mean(y * y, axis=-1, keepdims=True)
    h = _bf16(y * lax.rsqrt(ms + EPS) * g_ref[...])
    o_ref[...] = y
    for c in range(D_FF // ff_chunk):
        sl = slice(c * ff_chunk, (c + 1) * ff_chunk)
        gate = jnp.dot(h, wg_ref[:, sl], preferred_element_type=jnp.float32)
        up = jnp.dot(h, wu_ref[:, sl], preferred_element_type=jnp.float32)
        act = _bf16(gate * (1.0 / (1.0 + jnp.exp(-gate))) * up)
        o_ref[...] += jnp.dot(act, wd_ref[sl, :], preferred_element_type=jnp.float32)


def _ffn(x2d, moba, diff, dsa, wo, g, wg, wu, wd, tm, ff_chunk):
    m = x2d.shape[0]
    const = lambda i: (0, 0)
    resident = dict(pipeline_mode=pl.Buffered(1))
    return pl.pallas_call(
        functools.partial(_ffn_kernel, ff_chunk=ff_chunk),
        grid=(m // tm,),
        in_specs=[
            pl.BlockSpec((tm, D_MODEL), lambda i: (i, 0)),
            pl.BlockSpec((tm, 2 * LANES), lambda i: (i, 0)),
            pl.BlockSpec((tm, 4 * LANES), lambda i: (i, 0)),
            pl.BlockSpec((tm, 2 * LANES), lambda i: (i, 0)),
            pl.BlockSpec((D_MODEL, D_MODEL), const, **resident),
            pl.BlockSpec((1, D_MODEL), const),
            pl.BlockSpec((D_MODEL, D_FF), const, **resident),
            pl.BlockSpec((D_MODEL, D_FF), const, **resident),
            pl.BlockSpec((D_FF, D_MODEL), const, **resident),
        ],
        out_specs=pl.BlockSpec((tm, D_MODEL), lambda i: (i, 0)),
        out_shape=jax.ShapeDtypeStruct((m, D_MODEL), jnp.float32),
        compiler_params=pltpu.CompilerParams(
            dimension_semantics=("parallel",), vmem_limit_bytes=VMEM_LIMIT),
        name="outproj_ffn",
    )(x2d, moba, diff, dsa, wo, g, wg, wu, wd)


def _rope_tables(seq):
    inv = 1.0 / (ROPE_THETA ** (jnp.arange(0, HEAD_DIM, 2, dtype=jnp.float32) / HEAD_DIM))
    ang = jnp.arange(seq, dtype=jnp.float32)[:, None] * inv[None, :]
    cos, sin = jnp.cos(ang), jnp.sin(ang)
    return jnp.tile(cos, (1, 4)), jnp.concatenate([-sin, sin, -sin, sin], axis=1)


def _pack_w_in(w):
    ik = w[:, SLAB_IK * LANES:SLAB_IK * LANES + HEAD_DIM]
    iw = w[:, SLAB_IK * LANES + HEAD_DIM:IN_COLS]
    pad = jnp.zeros((w.shape[0], LANES - IDX_HEADS), w.dtype)
    return _bf16(jnp.concatenate([w[:, :SLAB_IK * LANES], ik, ik, iw, pad], axis=1))


def kernel(x, attn_norm_g, w_in, q_norm_g, k_norm_g, diff_lambda, diff_subln_g,
           w_out, ffn_norm_g, w_gate, w_up, w_down):
    b, seq, d = x.shape
    depth = w_in.shape[0]
    assert d == D_MODEL and seq % ROW_TILE == 0 and MOBA_BLOCK == VT_TILE
    tq = ROW_TILE
    tm_in = ROW_TILE
    tm_ffn = ROW_TILE
    cos_t, sin_t = _rope_tables(seq)
    x2d = x.reshape(b * seq, d)

    for l in range(depth):
        gains = jnp.stack([q_norm_g[l, 0], k_norm_g[l, 0], q_norm_g[l, 1], k_norm_g[l, 1],
                           q_norm_g[l, 2], k_norm_g[l, 2]])
        head_gains = jnp.concatenate(
            [jnp.tile(gains, (1, 2)), jnp.ones((2, LANES), jnp.float32)], axis=0)
        p, vt, vtd, iw = _inproj(x2d, attn_norm_g[l][None, :], _pack_w_in(w_in[l]), cos_t, sin_t,
                                 head_gains, seq, tm_in)
        p3 = p.reshape(b, seq, N_P_SLABS * LANES)
        iw3 = iw.reshape(b, seq, LANES)

        moba = _moba(p3, vt, seq)
        lam_init = 0.8 - 0.6 * math.exp(-0.3 * l)
        diff = _diff(p3, vtd, diff_lambda[l], diff_subln_g[l][None, :], seq, lam_init, tq)
        dsa = _dsa(p3, vt, iw3, seq, tq)

        x2d = _ffn(x2d, moba.reshape(b * seq, -1), diff.reshape(b * seq, -1), dsa.reshape(b * seq, -1),
                   _bf16(w_out[l]), ffn_norm_g[l][None, :], _bf16(w_gate[l]), _bf16(w_up[l]),
                   _bf16(w_down[l]), tm_ffn, 704)
    return x2d.reshape(b, seq, d)
```

```python
import functools
import math

import jax
import jax.numpy as jnp
from jax import lax
from jax.experimental import pallas as pl
from jax.experimental.pallas import tpu as pltpu

D_MODEL = 1024
HEAD_DIM = 64
MOBA_BLOCK = 256
MOBA_TOPK = 3
DSA_TOPK_MAX = 256
IDX_HEADS = 8
ROPE_THETA = 10000.0
EPS = 1e-6
D_FF = 2816

LANES = 128
HALF = HEAD_DIM // 2

SLAB_MOBA_Q, SLAB_MOBA_K, SLAB_MOBA_V = 0, 2, 4
SLAB_DIFF_Q, SLAB_DIFF_K, SLAB_DIFF_V = 6, 10, 14
SLAB_DSA_Q, SLAB_DSA_K, SLAB_DSA_V = 18, 20, 22
SLAB_IQ, SLAB_IK, SLAB_IW = 24, 28, 29
N_SLABS = 30
IN_COLS = 3656
SLABS_PER_DOT = 4

P_MOBA_Q, P_MOBA_K, P_DIFF_Q, P_DIFF_K, P_DSA_Q, P_DSA_K, P_IQ, P_IK = 0, 2, 4, 8, 12, 14, 16, 20
N_P_SLABS = 21
VT_MOBA, VT_DSA = 0, 4
N_VT_HEADS = 8
N_DIFF_HEADS = 4
VT_DIFF_ROWS = LANES + 16
VT_TILE = 256
ROW_TILE = 512
DIFF_ROW_TILE = 1024

ATTN_SCALE = HEAD_DIM ** -0.5 * math.log2(math.e)
IDX_SCALE = 64 ** -0.5

VMEM_LIMIT = 56 * 1024 * 1024
ACT_DTYPE = jnp.bfloat16

NEG_INF = float("-inf")
SEARCH_STEPS_PER_CHECK = 4
LOWEST_KEY = -2139095040


def _f32(x):
    return x.astype(jnp.float32)


def _bf16(x):
    return x.astype(ACT_DTYPE)


def _slab_kind(s):
    if s < SLAB_MOBA_K:
        return True, 0, True, ATTN_SCALE, ("p", P_MOBA_Q + s - SLAB_MOBA_Q)
    if s < SLAB_MOBA_V:
        return True, 1, True, 1.0, ("p", P_MOBA_K + s - SLAB_MOBA_K)
    if s < SLAB_DIFF_Q:
        return False, 0, False, 1.0, ("vt", VT_MOBA + 2 * (s - SLAB_MOBA_V))
    if s < SLAB_DIFF_K:
        return True, 2, True, ATTN_SCALE, ("p", P_DIFF_Q + s - SLAB_DIFF_Q)
    if s < SLAB_DIFF_V:
        return True, 3, True, 1.0, ("p", P_DIFF_K + s - SLAB_DIFF_K)
    if s < SLAB_DSA_Q:
        return False, 0, False, 1.0, ("vtd", s - SLAB_DIFF_V)
    if s < SLAB_DSA_K:
        return True, 4, True, ATTN_SCALE, ("p", P_DSA_Q + s - SLAB_DSA_Q)
    if s < SLAB_DSA_V:
        return True, 5, True, 1.0, ("p", P_DSA_K + s - SLAB_DSA_K)
    if s < SLAB_IQ:
        return False, 0, False, 1.0, ("vt", VT_DSA + 2 * (s - SLAB_DSA_V))
    if s < SLAB_IK:
        return False, 0, True, IDX_SCALE, ("p", P_IQ + s - SLAB_IQ)
    if s < SLAB_IW:
        return False, 0, True, 1.0, ("p", P_IK)
    return False, 0, False, IDX_HEADS ** -0.5, ("iw",)


def _inproj_kernel(x_ref, g_ref, w_ref, cos_ref, sin_ref, hg_ref, p_ref, vt_ref, vtd_ref, iw_ref):
    x = x_ref[...]
    ms = jnp.mean(x * x, axis=-1, keepdims=True)
    h = _bf16(x * lax.rsqrt(ms + EPS) * g_ref[...])

    lane = lax.broadcasted_iota(jnp.int32, (1, LANES), 1)
    first_half = (lane & (HEAD_DIM - 1)) < HALF
    r = lax.broadcasted_iota(jnp.int32, (LANES, LANES), 0) // HEAD_DIM
    c = lax.broadcasted_iota(jnp.int32, (LANES, LANES), 1) // HEAD_DIM
    head_ones = _bf16(jnp.where(r == c, 1.0, 0.0))
    cos = cos_ref[...]
    sin = sin_ref[...]
    hg = hg_ref[...]

    tm = x.shape[0]
    ones_rows = jnp.ones((HEAD_DIM, VT_TILE), vt_ref.dtype)
    for g0 in range(0, N_SLABS, SLABS_PER_DOT):
        n = min(SLABS_PER_DOT, N_SLABS - g0)
        yg = jnp.dot(h, w_ref[:, g0 * LANES:(g0 + n) * LANES], preferred_element_type=jnp.float32)
        for k in range(n):
            s = g0 + k
            y = yg[:, k * LANES:(k + 1) * LANES]
            norm, grow, rope, scale, dest = _slab_kind(s)
            if norm:
                yy = y * y
                hi = _bf16(yy)
                lo = _bf16(yy - _f32(hi))
                ss = (jnp.dot(hi, head_ones, preferred_element_type=jnp.float32)
                      + jnp.dot(lo, head_ones, preferred_element_type=jnp.float32))
                y = y * lax.rsqrt(ss * (1.0 / HEAD_DIM) + EPS) * hg[grow:grow + 1, :]
            if rope:
                partner = jnp.where(first_half, pltpu.roll(y, LANES - HALF, 1), pltpu.roll(y, HALF, 1))
                y = y * cos + partner * sin
            if scale != 1.0:
                y = y * scale
            if dest[0] == "p":
                p_ref[:, dest[1] * LANES:(dest[1] + 1) * LANES] = _bf16(y)
            elif dest[0] == "iw":
                iw_ref[...] = y
            else:
                head = dest[1]
                for r in range(tm // VT_TILE):
                    y_t = _bf16(y[r * VT_TILE:(r + 1) * VT_TILE, :].T)
                    if dest[0] == "vtd":
                        vtd_ref[head, r, 0:LANES, :] = y_t
                        vtd_ref[head, r, LANES:, :] = ones_rows[0:VT_DIFF_ROWS - LANES, :]
                    else:
                        for hh in range(2):
                            vt_ref[head + hh, r, 0:HEAD_DIM, :] = y_t[hh * HEAD_DIM:(hh + 1) * HEAD_DIM, :]
                            vt_ref[head + hh, r, HEAD_DIM:, :] = ones_rows


def _inproj(x2d, g, w_packed, cos_t, sin_t, head_gains, seq, tm):
    m = x2d.shape[0]
    nc = w_packed.shape[1]
    n_seq_tiles = seq // tm
    return pl.pallas_call(
        _inproj_kernel,
        grid=(m // tm,),
        in_specs=[
            pl.BlockSpec((tm, D_MODEL), lambda i: (i, 0)),
            pl.BlockSpec((1, D_MODEL), lambda i: (0, 0)),
            pl.BlockSpec((D_MODEL, nc), lambda i: (0, 0)),
            pl.BlockSpec((tm, LANES), lambda i: (i % n_seq_tiles, 0)),
            pl.BlockSpec((tm, LANES), lambda i: (i % n_seq_tiles, 0)),
            pl.BlockSpec((8, LANES), lambda i: (0, 0)),
        ],
        out_specs=[
            pl.BlockSpec((tm, N_P_SLABS * LANES), lambda i: (i, 0)),
            pl.BlockSpec((None, N_VT_HEADS, tm // VT_TILE, LANES, VT_TILE),
                         lambda i: (i // n_seq_tiles, 0, i % n_seq_tiles, 0, 0)),
            pl.BlockSpec((None, N_DIFF_HEADS, tm // VT_TILE, VT_DIFF_ROWS, VT_TILE),
                         lambda i: (i // n_seq_tiles, 0, i % n_seq_tiles, 0, 0)),
            pl.BlockSpec((tm, LANES), lambda i: (i, 0)),
        ],
        out_shape=[
            jax.ShapeDtypeStruct((m, N_P_SLABS * LANES), ACT_DTYPE),
            jax.ShapeDtypeStruct((m // seq, N_VT_HEADS, seq // VT_TILE, LANES, VT_TILE), ACT_DTYPE),
            jax.ShapeDtypeStruct((m // seq, N_DIFF_HEADS, seq // VT_TILE, VT_DIFF_ROWS, VT_TILE), ACT_DTYPE),
            jax.ShapeDtypeStruct((m, LANES), jnp.float32),
        ],
        compiler_params=pltpu.CompilerParams(
            dimension_semantics=("parallel",), vmem_limit_bytes=VMEM_LIMIT),
        name="inproj",
    )(x2d, g, w_packed, cos_t, sin_t, head_gains)


def _attend_pipelined(n_tiles, first_masked, n_maps, score_tile, query_keep, value_tile,
                      s_ref, p_ref, acc_ref):
    maps = range(n_maps)

    def put_scores(j, slot, masked):
        smax = []
        for c in maps:
            s = score_tile(j, c, masked)
            s_ref[slot, c] = s
            smax.append(jnp.where(query_keep(j, c), jnp.max(s, axis=0, keepdims=True), NEG_INF))
        return tuple(smax)

    def softmax(j, slot, smax, m_old):
        alpha, m_new = [], []
        for c in maps:
            m = jnp.maximum(m_old[c], smax[c])
            m_safe = jnp.where(m == NEG_INF, 0.0, m)
            alpha.append(jnp.exp2(m_old[c] - m_safe))
            p = jnp.exp2(s_ref[slot, c] - jnp.where(query_keep(j, c), m_safe, jnp.inf))
            p_ref[slot, c] = _bf16(p)
            m_new.append(m)
        return tuple(alpha), tuple(m_new)

    def add_values(j, slot, alpha):
        for c in maps:
            acc_ref[c] = alpha[c] * acc_ref[c] + jnp.dot(value_tile(j, c), p_ref[slot, c],
                                                         preferred_element_type=jnp.float32)

    def trip(masked, i, carry):
        smax0, alpha_prev, m = carry
        j = 2 * i
        smax1 = put_scores(j + 1, 1, masked)
        alpha0, m = softmax(j, 0, smax0, m)
        add_values(j - 1, 1, alpha_prev)
        smax0 = put_scores(j + 2, 0, masked)
        alpha1, m = softmax(j + 1, 1, smax1, m)
        add_values(j, 0, alpha0)
        return smax0, alpha1, m

    tk, tq = p_ref.shape[2:]
    m0 = (jnp.full((1, tq), NEG_INF, jnp.float32),) * n_maps
    one = (jnp.ones((1, tq), jnp.float32),) * n_maps
    for c in maps:
        p_ref[1, c] = jnp.zeros((tk, tq), p_ref.dtype)
    n_trips = (n_tiles + 1) // 2
    carry = (put_scores(0, 0, True), one, m0)
    n_plain = 0
    if first_masked is not None:
        n_plain = jnp.clip((first_masked - 1) // 2, 0, n_trips)
        carry = lax.fori_loop(0, n_plain, functools.partial(trip, False), carry)
    _, alpha_last, _ = lax.fori_loop(n_plain, n_trips, functools.partial(trip, True), carry)
    add_values(2 * n_trips - 1, 1, alpha_last)


def _causal_bias_tiles(tk, tq):
    key = lax.broadcasted_iota(jnp.int32, (tk, tq), 0)
    qry = lax.broadcasted_iota(jnp.int32, (tk, tq), 1)
    tiles = [jnp.zeros((tk, tq), jnp.float32)]
    for r in range(tq // tk):
        tiles.append(jnp.where(key + r * tk <= qry, 0.0, NEG_INF).astype(jnp.float32))
    return jnp.stack(tiles)


def _split_heads_in_slab(q):
    q_t = _f32(q).T
    row = lax.broadcasted_iota(jnp.int32, (LANES, 1), 0)
    return _bf16(jnp.where(row < HEAD_DIM, q_t, 0.0)), _bf16(jnp.where(row >= HEAD_DIM, q_t, 0.0))


def _moba_kernel(q_ref, k_ref, vt_ref, bias_ref, o_ref, kmean_ref, sel_ref, acc_ref, s_ref, p_ref, *, n_blocks):
    t = MOBA_BLOCK
    qi = pl.program_id(2)

    @pl.when(qi == 0)
    def _():
        kmean_ref[...] = jnp.zeros_like(kmean_ref)
        for j in range(n_blocks):
            kb = _f32(k_ref[j * t:(j + 1) * t, :])
            kmean_ref[j:j + 1, :] = jnp.sum(kb, axis=0, keepdims=True) * (1.0 / t)

    km = kmean_ref[...]
    km_hi = _bf16(km)
    km_lo = _bf16(km - _f32(km_hi))
    n_rows = kmean_ref.shape[0]
    blk = lax.broadcasted_iota(jnp.int32, (n_rows, t), 0)
    blk_f = _f32(blk)
    n_sel = min(MOBA_TOPK, n_blocks - 1)
    qhs = _split_heads_in_slab(q_ref[...])

    for hd, qh in enumerate(qhs):
        gate = (jnp.dot(km_hi, qh, preferred_element_type=jnp.float32)
                + jnp.dot(km_lo, qh, preferred_element_type=jnp.float32))
        gate = jnp.where(blk < qi, gate, NEG_INF)
        sel = jnp.zeros((n_rows, t), jnp.float32)
        for rank in range(n_sel):
            gmax = jnp.max(gate, axis=0, keepdims=True)
            first = jnp.min(jnp.where(gate == gmax, blk_f, float(n_rows)), axis=0, keepdims=True)
            pick = blk_f == first
            sel = jnp.where(pick, jnp.maximum(sel, jnp.where(rank < qi, 1.0, 0.0)), sel)
            gate = jnp.where(pick, NEG_INF, gate)
        sel_ref[hd] = jnp.where(blk == qi, 1.0, sel)
        acc_ref[hd] = jnp.zeros((LANES, t), jnp.float32)

    def score_tile(j, c, masked):
        kj = k_ref[pl.ds(pl.multiple_of(jnp.minimum(j, qi) * t, t), t), :]
        s = jnp.dot(kj, qhs[c], preferred_element_type=jnp.float32)
        return s + bias_ref[jnp.where(j >= qi, 1, 0)] if masked else s

    def query_keep(j, c):
        return sel_ref[c, pl.ds(jnp.minimum(j, n_rows - 1), 1), :] > 0.0

    def value_tile(j, c):
        return vt_ref[c, jnp.clip(j, 0, qi)]

    _attend_pipelined(qi + 1, qi, 2, score_tile, query_keep, value_tile, s_ref, p_ref, acc_ref)
    outs = [acc_ref[c, 0:HEAD_DIM, :] / acc_ref[c, HEAD_DIM:HEAD_DIM + 1, :] for c in range(2)]
    o_ref[...] = _bf16(jnp.concatenate(outs, axis=0).T)


def _moba(p3, vt, seq):
    b = p3.shape[0]
    t = MOBA_BLOCK
    n_blocks = seq // t
    n_rows = -(-(n_blocks + 2) // 8) * 8
    return pl.pallas_call(
        functools.partial(_moba_kernel, n_blocks=n_blocks),
        grid=(b, 2, n_blocks),
        in_specs=[
            pl.BlockSpec((None, t, LANES), lambda bi, s, qi: (bi, qi, P_MOBA_Q + s)),
            pl.BlockSpec((None, seq, LANES), lambda bi, s, qi: (bi, 0, P_MOBA_K + s)),
            pl.BlockSpec((None, 2, n_blocks, LANES, t), lambda bi, s, qi: (bi, VT_MOBA // 2 + s, 0, 0, 0)),
            pl.BlockSpec((2, t, t), lambda bi, s, qi: (0, 0, 0)),
        ],
        out_specs=pl.BlockSpec((None, t, LANES), lambda bi, s, qi: (bi, qi, s)),
        out_shape=jax.ShapeDtypeStruct((b, seq, 2 * LANES), ACT_DTYPE),
        scratch_shapes=[
            pltpu.VMEM((n_rows, LANES), jnp.float32),
            pltpu.VMEM((2, n_rows, t), jnp.float32),
            pltpu.VMEM((2, LANES, t), jnp.float32),
            pltpu.VMEM((2, 2, t, t), jnp.float32),
            pltpu.VMEM((2, 2, t, t), ACT_DTYPE),
        ],
        compiler_params=pltpu.CompilerParams(
            dimension_semantics=("parallel", "parallel", "arbitrary"), vmem_limit_bytes=VMEM_LIMIT),
        name="moba",
    )(p3, p3, vt, _causal_bias_tiles(t, t))


def _diff_kernel(lam_ref, q_ref, k_ref, vt_ref, sg_ref, bias_ref, o_ref, acc_ref, s_ref, p_ref, *, tq, lam_init):
    tk = VT_TILE
    qi = pl.program_id(2)
    per_q = tq // tk
    first_diag = qi * per_q
    last = first_diag + per_q - 1
    qs = _split_heads_in_slab(q_ref[...])

    def score_tile(j, c, masked):
        kj = k_ref[pl.ds(pl.multiple_of(jnp.minimum(j, last) * tk, tk), tk), :]
        s = jnp.dot(kj, qs[c], preferred_element_type=jnp.float32)
        return s + bias_ref[jnp.clip(j - first_diag + 1, 0, per_q)] if masked else s

    def query_keep(j, c):
        return jnp.broadcast_to(j <= last, (1, tq))

    def value_tile(j, c):
        return vt_ref[jnp.clip(j, 0, last)]

    for c in range(2):
        acc_ref[c] = jnp.zeros(acc_ref.shape[1:], jnp.float32)
    _attend_pipelined(last + 1, first_diag, 2, score_tile, query_keep, value_tile, s_ref, p_ref, acc_ref)

    lp = lam_ref[...]
    lam = (jnp.exp(jnp.sum(lp[0:1, :] * lp[1:2, :], axis=-1, keepdims=True))
           - jnp.exp(jnp.sum(lp[2:3, :] * lp[3:4, :], axis=-1, keepdims=True)) + lam_init)
    o1, o2 = (acc_ref[c, 0:LANES, :] / acc_ref[c, LANES:LANES + 1, :] for c in range(2))
    a = (o1 - lam * o2).T
    ms = jnp.mean(a * a, axis=-1, keepdims=True)
    o_ref[...] = _bf16(a * lax.rsqrt(ms + EPS) * sg_ref[...] * (1.0 - lam_init))


def _diff(p3, vt, lam_params, subln_g, seq, lam_init, tq):
    b = p3.shape[0]
    n_heads = N_DIFF_HEADS
    tk = VT_TILE
    bias = _causal_bias_tiles(tk, tq)
    return pl.pallas_call(
        functools.partial(_diff_kernel, tq=tq, lam_init=lam_init),
        grid=(b, n_heads, seq // tq),
        in_specs=[
            pl.BlockSpec((4, HEAD_DIM), lambda bi, h, qi: (0, 0)),
            pl.BlockSpec((None, tq, LANES), lambda bi, h, qi: (bi, qi, P_DIFF_Q + h)),
            pl.BlockSpec((None, seq, LANES), lambda bi, h, qi: (bi, 0, P_DIFF_K + h)),
            pl.BlockSpec((None, None, seq // tk, VT_DIFF_ROWS, tk), lambda bi, h, qi: (bi, h, 0, 0, 0)),
            pl.BlockSpec((1, LANES), lambda bi, h, qi: (0, 0)),
            pl.BlockSpec(bias.shape, lambda bi, h, qi: (0, 0, 0)),
        ],
        out_specs=pl.BlockSpec((None, tq, LANES), lambda bi, h, qi: (bi, qi, h)),
        out_shape=jax.ShapeDtypeStruct((b, seq, n_heads * LANES), ACT_DTYPE),
        scratch_shapes=[pltpu.VMEM((2, VT_DIFF_ROWS, tq), jnp.float32),
                        pltpu.VMEM((2, 2, tk, tq), jnp.float32),
                        pltpu.VMEM((2, 2, tk, tq), ACT_DTYPE)],
        compiler_params=pltpu.CompilerParams(
            dimension_semantics=("parallel", "parallel", "parallel"), vmem_limit_bytes=VMEM_LIMIT),
        name="diff",
    )(lam_params, p3, p3, vt, subln_g, bias)


def _key_to_f32(key):
    bits = key ^ ((key >> 31) & jnp.int32(0x7FFFFFFF))
    return lax.bitcast_convert_type(bits, jnp.float32)


def _dsa_kernel(iq_ref, ik_ref, iw_ref, q_ref, k_ref, vt_ref, o_ref,
                score_ref, acc_ref, j_ref, s_ref, p_ref, *, tq, n_top, idx_bits):
    t = VT_TILE
    qi = pl.program_id(1)
    per_q = tq // t
    first_diag = qi * per_q
    n_chunks = first_diag + per_q
    last = n_chunks - 1
    key_in_tile = lax.broadcasted_iota(jnp.int32, (t, tq), 0)
    qry_in_tile = lax.broadcasted_iota(jnp.int32, (t, tq), 1)

    iw_t = iw_ref[...].T
    iq_heads = []
    for slab in range(IDX_HEADS // 2):
        iq_heads.extend(_split_heads_in_slab(iq_ref[:, slab * LANES:(slab + 1) * LANES]))

    k_top = jnp.float32(n_top)
    lowest = _key_to_f32(jnp.int32(LOWEST_KEY))

    def index_chunk(c, diag, survey):
        kk = ik_ref[pl.ds(pl.multiple_of(c * t, t), t), :]
        sc = jnp.zeros((t, tq), jnp.float32)
        for h in range(IDX_HEADS):
            lg = jnp.dot(kk, iq_heads[h], preferred_element_type=jnp.float32)
            sc = sc + jnp.maximum(lg, 0.0) * iw_t[h:h + 1, :]
        if diag is not None:
            sc = jnp.where(key_in_tile + diag * t <= qry_in_tile, sc, NEG_INF)
        score_ref[c] = sc
        mx, n_fin, n_ge0, n_gt0 = survey
        fold = lambda x: x.reshape(t // 8, 8, tq)
        return (jnp.maximum(mx, jnp.max(fold(sc), axis=0)),
                n_fin + jnp.sum(fold(jnp.where(sc >= lowest, 1.0, 0.0)), axis=0),
                n_ge0 + jnp.sum(fold(jnp.where(sc >= 0.0, 1.0, 0.0)), axis=0),
                n_gt0 + jnp.sum(fold(jnp.where(sc > 0.0, 1.0, 0.0)), axis=0))

    zeros8 = jnp.zeros((8, tq), jnp.float32)
    survey = lax.fori_loop(0, first_diag, lambda c, sv: index_chunk(c, None, sv),
                           (jnp.full((8, tq), NEG_INF, jnp.float32), zeros8, zeros8, zeros8))
    for r in range(per_q):
        survey = index_chunk(first_diag + r, r, survey)
    mx = jnp.max(survey[0], axis=0, keepdims=True)
    n_fin, n_ge0, n_gt0 = (jnp.sum(v, axis=0, keepdims=True) for v in survey[1:])

    if per_q % 2 == 1:
        @pl.when(n_chunks % 2 == 1)
        def _():
            score_ref[n_chunks] = jnp.full((t, tq), NEG_INF, jnp.float32)

    n_pairs = (n_chunks + 1) // 2

    def count_keys(hits):
        def body(i, acc):
            for c in (2 * i, 2 * i + 1):
                acc = acc + jnp.sum(hits(score_ref[c], c).reshape(t // 8, 8, tq), axis=0)
            return acc
        acc = lax.fori_loop(0, n_pairs, body, jnp.zeros((8, tq), jnp.float32))
        return jnp.sum(acc, axis=0, keepdims=True)

    def count_ge(cand_key):
        cand = _key_to_f32(cand_key)
        return count_keys(lambda s, c: jnp.where(s >= cand, 1.0, 0.0))

    mx_bits = lax.bitcast_convert_type(mx, jnp.int32)
    mx_key = mx_bits ^ ((mx_bits >> 31) & jnp.int32(0x7FFFFFFF))
    few = n_fin <= k_top
    at_zero = n_ge0 >= k_top
    zero_key = jnp.zeros((1, tq), jnp.int32)
    lo = jnp.where(few, LOWEST_KEY, jnp.where(at_zero, zero_key, LOWEST_KEY))
    c_lo = jnp.where(few, n_fin, jnp.where(at_zero, n_ge0, n_fin))
    hi = jnp.where(at_zero, jnp.where(n_gt0 < k_top, zero_key + 1, mx_key + 1), zero_key)
    hi = jnp.where(few, lo + 1, hi)

    def still_active(lo, hi, c_lo):
        return jnp.where(c_lo == k_top, 0.0, jnp.where(hi - lo <= 1, 0.0, 1.0))

    def search_cond(state):
        it, n_active = state[0], state[1]
        return jnp.logical_and(n_active > 0.0, it < 34)

    def search_step(state):
        it, _, lo, hi, c_lo = state
        for _ in range(SEARCH_STEPS_PER_CHECK):
            active = still_active(lo, hi, c_lo) > 0.0
            below = lax.bitcast_convert_type(_key_to_f32(hi - 1) * 0.0625, jnp.int32)
            cand = jnp.where(lo == 0, below, lo + ((hi - lo) >> 1))
            cand = jnp.minimum(jnp.maximum(cand, lo + 1), hi - 1)
            cnt = count_ge(cand)
            ge = cnt >= k_top
            lo, hi, c_lo = (jnp.where(active, jnp.where(ge, cand, lo), lo),
                            jnp.where(active, jnp.where(ge, hi, cand), hi),
                            jnp.where(active, jnp.where(ge, cnt, c_lo), c_lo))
        return it + SEARCH_STEPS_PER_CHECK, jnp.sum(still_active(lo, hi, c_lo)), lo, hi, c_lo

    _, _, lo, hi, c_lo = lax.while_loop(
        search_cond, search_step, (jnp.int32(0), jnp.sum(still_active(lo, hi, c_lo)), lo, hi, c_lo))
    thr = _key_to_f32(lo)

    j_ref[...] = jnp.full(j_ref.shape, 2 ** 30, jnp.int32)

    @pl.when(jnp.max(jnp.where(c_lo > k_top, 1.0, 0.0)) > 0.0)
    def _():
        need = k_top - count_keys(lambda s, c: jnp.where(s > thr, 1.0, 0.0))

        def jbisect(it, lo):
            cand = lo + (jnp.int32(1) << (idx_bits - 1 - it))
            cnt = count_keys(lambda s, c: jnp.where(
                s == thr, jnp.where((key_in_tile + c * t) <= cand, 1.0, 0.0), 0.0))
            return jnp.where(cnt < need, cand, lo)
        lo = lax.fori_loop(0, idx_bits, jbisect, jnp.full((1, tq), -1, jnp.int32))
        j_ref[...] = jnp.broadcast_to(lo + 1, j_ref.shape)

    j_last = j_ref[0:1, :]

    n_heads = 4
    qhs = []
    for slab in range(n_heads // 2):
        qhs.extend(_split_heads_in_slab(q_ref[:, slab * LANES:(slab + 1) * LANES]))
    for hd in range(n_heads):
        acc_ref[hd] = jnp.zeros((LANES, tq), jnp.float32)

    shared = {}

    def score_tile(j, hd, masked):
        jc = jnp.minimum(j, last)
        if hd == 0:
            s_idx = score_ref[jc]
            keep_tie = jnp.where((key_in_tile + jc * t) <= j_last, 0.0, NEG_INF)
            shared["bias"] = jnp.where(s_idx > thr, 0.0, jnp.where(s_idx == thr, keep_tie, NEG_INF))
        slab = hd // 2
        kj = k_ref[pl.ds(pl.multiple_of(jc * t, t), t), slab * LANES:(slab + 1) * LANES]
        return jnp.dot(kj, qhs[hd], preferred_element_type=jnp.float32) + shared["bias"]

    def query_keep(j, hd):
        return jnp.broadcast_to(j <= last, (1, tq))

    def value_tile(j, hd):
        return vt_ref[hd, jnp.clip(j, 0, last)]

    _attend_pipelined(n_chunks, None, n_heads, score_tile, query_keep, value_tile, s_ref, p_ref, acc_ref)
    outs = [acc_ref[hd, 0:HEAD_DIM, :] / acc_ref[hd, HEAD_DIM:HEAD_DIM + 1, :] for hd in range(n_heads)]
    for slab in range(n_heads // 2):
        o_ref[:, slab * LANES:(slab + 1) * LANES] = _bf16(jnp.concatenate(outs[2 * slab:2 * slab + 2], axis=0).T)


def _dsa(p3, vt, iw3, seq, tq):
    b = p3.shape[0]
    t = VT_TILE
    n_top = min(DSA_TOPK_MAX, seq // 4)
    idx_bits = max(1, (seq - 1).bit_length())
    n_tiles = seq // t
    n_score_tiles = n_tiles + (tq // t) % 2
    return pl.pallas_call(
        functools.partial(_dsa_kernel, tq=tq, n_top=n_top, idx_bits=idx_bits),
        grid=(b, seq // tq),
        in_specs=[
            pl.BlockSpec((None, tq, 4 * LANES), lambda bi, qi: (bi, qi, P_IQ // 4)),
            pl.BlockSpec((None, seq, LANES), lambda bi, qi: (bi, 0, P_IK)),
            pl.BlockSpec((None, tq, LANES), lambda bi, qi: (bi, qi, 0)),
            pl.BlockSpec((None, tq, 2 * LANES), lambda bi, qi: (bi, qi, P_DSA_Q // 2)),
            pl.BlockSpec((None, seq, 2 * LANES), lambda bi, qi: (bi, 0, P_DSA_K // 2)),
            pl.BlockSpec((None, 4, n_tiles, LANES, t), lambda bi, qi: (bi, VT_DSA // 4, 0, 0, 0)),
        ],
        out_specs=pl.BlockSpec((None, tq, 2 * LANES), lambda bi, qi: (bi, qi, 0)),
        out_shape=jax.ShapeDtypeStruct((b, seq, 2 * LANES), ACT_DTYPE),
        scratch_shapes=[
            pltpu.VMEM((n_score_tiles, t, tq), jnp.float32),
            pltpu.VMEM((4, LANES, tq), jnp.float32),
            pltpu.VMEM((8, tq), jnp.int32),
            pltpu.VMEM((2, 4, t, tq), jnp.float32),
            pltpu.VMEM((2, 4, t, tq), ACT_DTYPE),
        ],
        compiler_params=pltpu.CompilerParams(
            dimension_semantics=("parallel", "parallel"), vmem_limit_bytes=VMEM_LIMIT),
        name="dsa",
    )(p3, p3, iw3, p3, p3, vt)


def _ffn_kernel(x_ref, moba_ref, diff_ref, dsa_ref, wo_ref, g_ref, wg_ref, wu_ref, wd_ref, o_ref, *, ff_chunk):
    w_moba, w_diff = 2 * LANES, 4 * LANES
    y = (x_ref[...]
         + jnp.dot(moba_ref[...], wo_ref[0:w_moba, :], preferred_element_type=jnp.float32)
         + jnp.dot(diff_ref[...], wo_ref[w_moba:w_moba + w_diff, :], preferred_element_type=jnp.float32)
         + jnp.dot(dsa_ref[...], wo_ref[w_moba + w_diff:, :], preferred_element_type=jnp.float32))
    ms = jnp.mean(y * y, axis=-1, keepdims=True)
    h = _bf16(y * lax.rsqrt(ms + EPS) * g_ref[...])
    o_ref[...] = y
    for c in range(D_FF // ff_chunk):
        sl = slice(c * ff_chunk, (c + 1) * ff_chunk)
        gate = jnp.dot(h, wg_ref[:, sl], preferred_element_type=jnp.float32)
        up = jnp.dot(h, wu_ref[:, sl], preferred_element_type=jnp.float32)
        act = _bf16(gate * (1.0 / (1.0 + jnp.exp(-gate))) * up)
        o_ref[...] += jnp.dot(act, wd_ref[sl, :], preferred_element_type=jnp.float32)


def _ffn(x2d, moba, diff, dsa, wo, g, wg, wu, wd, tm, ff_chunk):
    m = x2d.shape[0]
    const = lambda i: (0, 0)
    resident = dict(pipeline_mode=pl.Buffered(1))
    return pl.pallas_call(
        functools.partial(_ffn_kernel, ff_chunk=ff_chunk),
        grid=(m // tm,),
        in_specs=[
            pl.BlockSpec((tm, D_MODEL), lambda i: (i, 0)),
            pl.BlockSpec((tm, 2 * LANES), lambda i: (i, 0)),
            pl.BlockSpec((tm, 4 * LANES), lambda i: (i, 0)),
            pl.BlockSpec((tm, 2 * LANES), lambda i: (i, 0)),
            pl.BlockSpec((D_MODEL, D_MODEL), const, **resident),
            pl.BlockSpec((1, D_MODEL), const),
            pl.BlockSpec((D_MODEL, D_FF), const, **resident),
            pl.BlockSpec((D_MODEL, D_FF), const, **resident),
            pl.BlockSpec((D_FF, D_MODEL), const, **resident),
        ],
        out_specs=pl.BlockSpec((tm, D_MODEL), lambda i: (i, 0)),
        out_shape=jax.ShapeDtypeStruct((m, D_MODEL), jnp.float32),
        compiler_params=pltpu.CompilerParams(
            dimension_semantics=("parallel",), vmem_limit_bytes=VMEM_LIMIT),
        name="outproj_ffn",
    )(x2d, moba, diff, dsa, wo, g, wg, wu, wd)


def _rope_tables(seq):
    inv = 1.0 / (ROPE_THETA ** (jnp.arange(0, HEAD_DIM, 2, dtype=jnp.float32) / HEAD_DIM))
    ang = jnp.arange(seq, dtype=jnp.float32)[:, None] * inv[None, :]
    cos, sin = jnp.cos(ang), jnp.sin(ang)
    return jnp.tile(cos, (1, 4)), jnp.concatenate([-sin, sin, -sin, sin], axis=1)


def _pack_w_in(w):
    ik = w[:, SLAB_IK * LANES:SLAB_IK * LANES + HEAD_DIM]
    iw = w[:, SLAB_IK * LANES + HEAD_DIM:IN_COLS]
    pad = jnp.zeros((w.shape[0], LANES - IDX_HEADS), w.dtype)
    return _bf16(jnp.concatenate([w[:, :SLAB_IK * LANES], ik, ik, iw, pad], axis=1))


def kernel(x, attn_norm_g, w_in, q_norm_g, k_norm_g, diff_lambda, diff_subln_g,
           w_out, ffn_norm_g, w_gate, w_up, w_down):
    b, seq, d = x.shape
    depth = w_in.shape[0]
    assert d == D_MODEL and seq % ROW_TILE == 0 and MOBA_BLOCK == VT_TILE
    tq = ROW_TILE
    tq_diff = DIFF_ROW_TILE if seq % DIFF_ROW_TILE == 0 else ROW_TILE
    tm_in = ROW_TILE
    tm_ffn = ROW_TILE
    cos_t, sin_t = _rope_tables(seq)
    x2d = x.reshape(b * seq, d)

    for l in range(depth):
        gains = jnp.stack([q_norm_g[l, 0], k_norm_g[l, 0], q_norm_g[l, 1], k_norm_g[l, 1],
                           q_norm_g[l, 2], k_norm_g[l, 2]])
        head_gains = jnp.concatenate(
            [jnp.tile(gains, (1, 2)), jnp.ones((2, LANES), jnp.float32)], axis=0)
        p, vt, vtd, iw = _inproj(x2d, attn_norm_g[l][None, :], _pack_w_in(w_in[l]), cos_t, sin_t,
                                 head_gains, seq, tm_in)
        p3 = p.reshape(b, seq, N_P_SLABS * LANES)
        iw3 = iw.reshape(b, seq, LANES)

        moba = _moba(p3, vt, seq)
        lam_init = 0.8 - 0.6 * math.exp(-0.3 * l)
        diff = _diff(p3, vtd, diff_lambda[l], diff_subln_g[l][None, :], seq, lam_init, tq_diff)
        dsa = _dsa(p3, vt, iw3, seq, tq)

        x2d = _ffn(x2d, moba.reshape(b * seq, -1), diff.reshape(b * seq, -1), dsa.reshape(b * seq, -1),
                   _bf16(w_out[l]), ffn_norm_g[l][None, :], _bf16(w_gate[l]), _bf16(w_up[l]),
                   _bf16(w_down[l]), tm_ffn, 704)
    return x2d.reshape(b, seq, d)
```
